```python
import math
import jax, jax.numpy as jnp
from jax import lax
import numpy as np

D_MODEL = 2048
BATCH = 2
SEQ = 4096
DEPTH = 4
DEC_BATCH = 8
DEC_SEQ = 4
PAST_LEN = 16384
PAGE_SIZE = 128

CONV_CH = D_MODEL // 2
CONV_WIDTH = 31
SB_HEADS = 8
SB_HEAD_DIM = D_MODEL // (2 * SB_HEADS)
SB_WIDTH = SB_HEADS * SB_HEAD_DIM
SB_BLOCK = 128
SB_BIAS_INIT = -8.0
AB_IN = 2 * CONV_CH + 3 * SB_WIDTH
AB_OUT = CONV_CH + SB_WIDTH
GDN_HEAD_DIM = 128
GDN_K_HEADS = D_MODEL // GDN_HEAD_DIM
GDN_V_HEADS = 2 * GDN_K_HEADS
GDN_KW = GDN_K_HEADS * GDN_HEAD_DIM
GDN_VW = GDN_V_HEADS * GDN_HEAD_DIM
GDN_QKV = 2 * GDN_KW + GDN_VW
GDN_IN = GDN_QKV + GDN_VW + 2 * GDN_V_HEADS
GDN_CONV = 4
GDN_CHUNK = 64
FFN_DIM = 256 * (-(-8 * D_MODEL // (3 * 256)))
FFN_RESIDUAL = 0.5
N_SB = (DEPTH + 1) // 2
N_GDN = DEPTH // 2
EPS = 1e-6

kernel_name = "hybrid_conformer_stickbreaking_gdn_decoder_step"


def rms_norm(x, g):
    xf = x.astype(jnp.float32)
    y = xf * lax.rsqrt(jnp.mean(xf * xf, axis=-1, keepdims=True) + EPS)
    return y.astype(x.dtype) * g


def layer_norm(x, g, b):
    xf = x.astype(jnp.float32)
    mu = jnp.mean(xf, axis=-1, keepdims=True)
    var = jnp.mean(jnp.square(xf - mu), axis=-1, keepdims=True)
    return ((xf - mu) * lax.rsqrt(var + EPS)).astype(x.dtype) * g + b


def l2_normalize(x):
    xf = x.astype(jnp.float32)
    return xf * lax.rsqrt(jnp.sum(xf * xf, axis=-1, keepdims=True) + EPS)


def swiglu(h, w_gate, w_up, w_down):
    return (jax.nn.silu(h @ w_gate) * (h @ w_up)) @ w_down


def causal_depthwise_conv(x_ext, w):
    return lax.conv_general_dilated(x_ext, w[:, None, :].astype(x_ext.dtype), window_strides=(1,),
                                    padding='VALID', dimension_numbers=('NWC', 'WIO', 'NWC'),
                                    feature_group_count=x_ext.shape[-1])


def stick_breaking_attention(q, k, v, bias, q_offset):
    B, T, H, Dh = q.shape
    S = k.shape[1]
    blk = math.gcd(T, SB_BLOCK)
    nb = T // blk
    k_pos = jnp.arange(S, dtype=jnp.int32)
    q_blocks = q.reshape(B, nb, blk, H, Dh).swapaxes(0, 1)
    q_pos = (q_offset + jnp.arange(T, dtype=jnp.int32)).reshape(nb, blk)
    bias_f = bias.astype(jnp.float32)[None, :, None, None]

    def block(args):
        qb, pb = args
        z = jnp.einsum('bqhd,bshd->bhqs', qb, k).astype(jnp.float32) * (Dh ** -0.5) + bias_f
        visible = k_pos[None, :] < pb[:, None]
        log_beta = jnp.where(visible, jax.nn.log_sigmoid(z), -jnp.inf)
        log_keep = jnp.where(visible, jax.nn.log_sigmoid(-z), 0.0)
        later = lax.cumsum(log_keep, axis=3, reverse=True) - log_keep
        w = jnp.exp(log_beta + later)
        return jnp.einsum('bhqs,bshd->bqhd', w.astype(v.dtype), v)

    out = lax.map(block, (q_blocks, q_pos))
    return out.swapaxes(0, 1).reshape(B, T, H, Dh)


def gated_delta_rule(q, k, v, g, beta, s0):
    B, T, H, DK = q.shape
    DV = v.shape[-1]
    C = math.gcd(T, GDN_CHUNK)
    N = T // C

    def heads_first(t):
        t = t.reshape(B, N, C, *t.shape[2:])
        return jnp.moveaxis(t, 3, 2)

    q, k, v, g, beta = (heads_first(t) for t in (q, k, v, g, beta))
    gc = jnp.cumsum(g, axis=-1)
    causal = jnp.tri(C, dtype=bool)
    strict = jnp.tri(C, k=-1, dtype=bool)
    decay = jnp.exp(jnp.where(causal, gc[..., :, None] - gc[..., None, :], -jnp.inf))
    kb = k * beta[..., None]
    kkt = jnp.einsum('bnhid,bnhjd->bnhij', kb, k) * decay
    tmat = jnp.eye(C, dtype=jnp.float32) + jnp.where(strict, kkt, 0.0)
    rhs = jnp.concatenate([v * beta[..., None], kb * jnp.exp(gc)[..., None]], axis=-1)
    sol = lax.linalg.triangular_solve(tmat, rhs, left_side=True, lower=True, unit_diagonal=True)
    u, w = sol[..., :DV], sol[..., DV:]
    attn = jnp.einsum('bnhid,bnhjd->bnhij', q, k) * decay
    q_dec = q * jnp.exp(gc)[..., None]
    k_dec = k * jnp.exp(gc[..., -1:] - gc)[..., None]
    g_last = jnp.exp(gc[..., -1])

    def step(S, xs):
        u_n, w_n, attn_n, qd_n, kd_n, gl_n = xs
        v_new = u_n - jnp.einsum('bhck,bhkv->bhcv', w_n, S)
        o = jnp.einsum('bhck,bhkv->bhcv', qd_n, S) + jnp.einsum('bhij,bhjv->bhiv', attn_n, v_new)
        S = S * gl_n[..., None, None] + jnp.einsum('bhck,bhcv->bhkv', kd_n, v_new)
        return S, o

    xs = tuple(t.swapaxes(0, 1) for t in (u, w, attn, q_dec, k_dec, g_last))
    S, o = lax.scan(step, s0, xs)
    o = jnp.moveaxis(o.swapaxes(0, 1), 2, 3).reshape(B, T, H, DV)
    return o, S


def conv_sb_mixer(h, conv_state, past_kv, w_in, conv_w, conv_b, ln_g, ln_b, w_out, sb_bias):
    B, T, _ = h.shape
    a_val, a_gate, q, k, v = jnp.split(
        h @ w_in, [CONV_CH, 2 * CONV_CH, 2 * CONV_CH + SB_WIDTH, 2 * CONV_CH + 2 * SB_WIDTH], axis=-1)
    a = a_val * jax.nn.sigmoid(a_gate)
    a_ext = jnp.concatenate([conv_state, a], axis=1)
    new_conv_state = a_ext[:, T:]
    a = causal_depthwise_conv(a_ext, conv_w) + conv_b
    a = jax.nn.silu(layer_norm(a, ln_g, ln_b))
    q = q.reshape(B, T, SB_HEADS, SB_HEAD_DIM)
    kv_new = jnp.stack([k.reshape(B, T, SB_HEADS, SB_HEAD_DIM),
                        v.reshape(B, T, SB_HEADS, SB_HEAD_DIM)], axis=2)
    kv_all = jnp.concatenate([past_kv.astype(kv_new.dtype), kv_new], axis=1)
    o = stick_breaking_attention(q, kv_all[:, :, 0], kv_all[:, :, 1], sb_bias, past_kv.shape[1])
    y = jnp.concatenate([a, o.reshape(B, T, SB_WIDTH)], axis=-1) @ w_out
    return y, new_conv_state, kv_new


def gdn_mixer(h, conv_state, s0, w_in, conv_w, a_log, dt_bias, norm_g, w_out):
    B, T, _ = h.shape
    qkv, z, b, a = jnp.split(h @ w_in, [GDN_QKV, GDN_QKV + GDN_VW, GDN_QKV + GDN_VW + GDN_V_HEADS], axis=-1)
    qkv_ext = jnp.concatenate([conv_state, qkv], axis=1)
    new_conv_state = qkv_ext[:, T:]
    qkv = jax.nn.silu(causal_depthwise_conv(qkv_ext, conv_w))
    q, k, v = jnp.split(qkv, [GDN_KW, 2 * GDN_KW], axis=-1)
    rep = GDN_V_HEADS // GDN_K_HEADS
    q = jnp.repeat(l2_normalize(q.reshape(B, T, GDN_K_HEADS, GDN_HEAD_DIM)) * (GDN_HEAD_DIM ** -0.5), rep, axis=2)
    k = jnp.repeat(l2_normalize(k.reshape(B, T, GDN_K_HEADS, GDN_HEAD_DIM)), rep, axis=2)
    v = v.reshape(B, T, GDN_V_HEADS, GDN_HEAD_DIM).astype(jnp.float32)
    beta = jax.nn.sigmoid(b.astype(jnp.float32))
    g = -jnp.exp(a_log.astype(jnp.float32)) * jax.nn.softplus(a.astype(jnp.float32) + dt_bias.astype(jnp.float32))
    o, s = gated_delta_rule(q, k, v, g, beta, s0.astype(jnp.float32))
    o = rms_norm(o, norm_g.astype(jnp.float32)) * jax.nn.silu(z.reshape(B, T, GDN_V_HEADS, GDN_HEAD_DIM).astype(jnp.float32))
    y = o.reshape(B, T, GDN_VW).astype(h.dtype) @ w_out
    return y, new_conv_state, s.astype(s0.dtype)


def modulate(x, g_pre, shift, scale):
    return rms_norm(x, g_pre) * (1 + scale) + shift


def gated_residual(x, y, g_post, gate, weight):
    return x + weight * (1 + gate) * rms_norm(y, g_post)


def trunk(x, c, cache_kv, table, conv_a_state, conv_c_state, gdn_state,
          w_ada, b_ada, norm_pre, norm_post, w_ffn_gate, w_ffn_up, w_ffn_down,
          w_in_ab, conv_a_w, conv_a_b, ln_a_g, ln_a_b, w_out_ab, sb_bias,
          w_in_gdn, conv_c_w, a_log, dt_bias, gdn_norm_g, w_out_gdn):
    B = x.shape[0]
    silu_c = jax.nn.silu(c)
    kv_rows, conv_a_new, conv_c_new, gdn_new = [], [], [], []
    for l in range(DEPTH):
        mod = (silu_c @ w_ada[l] + b_ada[l]).reshape(B, 3, 3, 1, D_MODEL)
        h = modulate(x, norm_pre[l, 0], mod[:, 0, 0], mod[:, 0, 1])
        x = gated_residual(x, swiglu(h, w_ffn_gate[l, 0], w_ffn_up[l, 0], w_ffn_down[l, 0]),
                           norm_post[l, 0], mod[:, 0, 2], FFN_RESIDUAL)
        h = modulate(x, norm_pre[l, 1], mod[:, 1, 0], mod[:, 1, 1])
        i = l // 2
        if l % 2 == 0:
            past = cache_kv[i][table].reshape(B, table.shape[1] * PAGE_SIZE, 2, SB_HEADS, SB_HEAD_DIM)
            y, cs, kv = conv_sb_mixer(h, conv_a_state[i], past, w_in_ab[i], conv_a_w[i], conv_a_b[i],
                                      ln_a_g[i], ln_a_b[i], w_out_ab[i], sb_bias[i])
            conv_a_new.append(cs)
            kv_rows.append(kv)
        else:
            y, cs, s = gdn_mixer(h, conv_c_state[i], gdn_state[i], w_in_gdn[i], conv_c_w[i], a_log[i],
                                 dt_bias[i], gdn_norm_g[i], w_out_gdn[i])
            conv_c_new.append(cs)
            gdn_new.append(s)
        x = gated_residual(x, y, norm_post[l, 1], mod[:, 1, 2], 1.0)
        h = modulate(x, norm_pre[l, 2], mod[:, 2, 0], mod[:, 2, 1])
        x = gated_residual(x, swiglu(h, w_ffn_gate[l, 1], w_ffn_up[l, 1], w_ffn_down[l, 1]),
                           norm_post[l, 2], mod[:, 2, 2], FFN_RESIDUAL)
    return x, jnp.stack(kv_rows), jnp.stack(conv_a_new), jnp.stack(conv_c_new), jnp.stack(gdn_new)


def setup_inputs(seed: int = 0) -> dict:
    key = jax.random.key(seed)
    ks = iter(jax.random.split(key, 40))

    def nrm(shape, s):
        return jax.random.normal(next(ks), shape, jnp.float32) * s

    n_pages = PAST_LEN // PAGE_SIZE
    used = DEC_BATCH * n_pages
    n_pool = used + max(1, used // 4)
    page_table = jax.random.permutation(next(ks), n_pool)[:used].reshape(DEC_BATCH, n_pages).astype(jnp.int32)
    D = D_MODEL
    return {
        'x_prompt': nrm((BATCH, SEQ, D), 1.0),
        'x_sample': nrm((DEC_BATCH, DEC_SEQ, D), 1.0),
        'cache_kv_sb': nrm((N_SB, n_pool, PAGE_SIZE, 2, SB_HEADS, SB_HEAD_DIM), 1.0),
        'state_conv_a': nrm((N_SB, DEC_BATCH, CONV_WIDTH - 1, CONV_CH), 0.5),
        'state_conv_c': nrm((N_GDN, DEC_BATCH, GDN_CONV - 1, GDN_QKV), 1.0),
        'state_gdn': nrm((N_GDN, DEC_BATCH, GDN_V_HEADS, GDN_HEAD_DIM, GDN_HEAD_DIM), 0.1),
        'page_table': page_table,
        'c_prompt': nrm((BATCH, D), 1.0),
        'c_sample': nrm((DEC_BATCH, D), 1.0),
        'w_ada': nrm((DEPTH, D, 9 * D), 0.5 * D ** -0.5),
        'b_ada': nrm((DEPTH, 9 * D), 0.01),
        'norm_pre': 1.0 + nrm((DEPTH, 3, D), 0.05),
        'norm_post': 1.0 + nrm((DEPTH, 3, D), 0.05),
        'w_ffn_gate': nrm((DEPTH, 2, D, FFN_DIM), D ** -0.5),
        'w_ffn_up': nrm((DEPTH, 2, D, FFN_DIM), D ** -0.5),
        'w_ffn_down': nrm((DEPTH, 2, FFN_DIM, D), FFN_DIM ** -0.5),
        'w_in_ab': nrm((N_SB, D, AB_IN), D ** -0.5),
        'conv_a_w': nrm((N_SB, CONV_WIDTH, CONV_CH), CONV_WIDTH ** -0.5),
        'conv_a_b': nrm((N_SB, CONV_CH), 0.01),
        'ln_a_g': 1.0 + nrm((N_SB, CONV_CH), 0.05),
        'ln_a_b': nrm((N_SB, CONV_CH), 0.01),
        'w_out_ab': nrm((N_SB, AB_OUT, D), AB_OUT ** -0.5),
        'sb_bias': SB_BIAS_INIT + nrm((N_SB, SB_HEADS), 0.5),
        'w_in_gdn': nrm((N_GDN, D, GDN_IN), D ** -0.5),
        'conv_c_w': nrm((N_GDN, GDN_CONV, GDN_QKV), GDN_CONV ** -0.5),
        'a_log': jnp.log(jax.random.uniform(next(ks), (N_GDN, GDN_V_HEADS), jnp.float32, 1.0, 16.0)),
        'dt_bias': nrm((N_GDN, GDN_V_HEADS), 0.1),
        'gdn_norm_g': 1.0 + nrm((N_GDN, GDN_HEAD_DIM), 0.05),
        'w_out_gdn': nrm((N_GDN, GDN_VW, D), GDN_VW ** -0.5),
    }


def reference(x_prompt, x_sample, cache_kv_sb, state_conv_a, state_conv_c, state_gdn, page_table,
              c_prompt, c_sample, w_ada, b_ada, norm_pre, norm_post, w_ffn_gate, w_ffn_up, w_ffn_down,
              w_in_ab, conv_a_w, conv_a_b, ln_a_g, ln_a_b, w_out_ab, sb_bias,
              w_in_gdn, conv_c_w, a_log, dt_bias, gdn_norm_g, w_out_gdn):
    dt = x_prompt.dtype
    bp = x_prompt.shape[0]
    y_prompt, kv_p, conv_a_p, conv_c_p, gdn_p = trunk(
        x_prompt, c_prompt, cache_kv_sb, jnp.zeros((bp, 0), jnp.int32),
        jnp.zeros((N_SB, bp, CONV_WIDTH - 1, CONV_CH), dt),
        jnp.zeros((N_GDN, bp, GDN_CONV - 1, GDN_QKV), dt),
        jnp.zeros((N_GDN, bp, GDN_V_HEADS, GDN_HEAD_DIM, GDN_HEAD_DIM), dt),
        w_ada, b_ada, norm_pre, norm_post, w_ffn_gate, w_ffn_up, w_ffn_down,
        w_in_ab, conv_a_w, conv_a_b, ln_a_g, ln_a_b, w_out_ab, sb_bias,
        w_in_gdn, conv_c_w, a_log, dt_bias, gdn_norm_g, w_out_gdn)
    y_sample, kv_s, conv_a_s, conv_c_s, gdn_s = trunk(
        x_sample, c_sample, cache_kv_sb, page_table, state_conv_a, state_conv_c, state_gdn,
        w_ada, b_ada, norm_pre, norm_post, w_ffn_gate, w_ffn_up, w_ffn_down,
        w_in_ab, conv_a_w, conv_a_b, ln_a_g, ln_a_b, w_out_ab, sb_bias,
        w_in_gdn, conv_c_w, a_log, dt_bias, gdn_norm_g, w_out_gdn)
    return (y_prompt, y_sample, kv_p, conv_a_p, conv_c_p, gdn_p, kv_s, conv_a_s, conv_c_s, gdn_s)
```

```python
import functools
import math

import jax
import jax.numpy as jnp
from jax import lax
from jax.experimental import pallas as pl
from jax.experimental.pallas import tpu as pltpu

F32 = jnp.float32
BF16 = jnp.bfloat16
EPS = 1e-6
FFN_RESIDUAL = 0.5
VMEM_LIMIT = 56 * 1024 * 1024


def _cparams(sem):
    return pltpu.CompilerParams(dimension_semantics=sem, vmem_limit_bytes=VMEM_LIMIT)


def _silu(x):
    return x * jax.nn.sigmoid(x)


def _rms(x):
    return x * lax.rsqrt(jnp.mean(x * x, axis=-1, keepdims=True) + EPS)


def _ada_kernel(c_ref, w_ref, b_ref, o_ref):
    s = _silu(c_ref[...]).astype(BF16)
    o_ref[...] = jnp.dot(s, w_ref[...].astype(BF16), preferred_element_type=F32) + b_ref[...]


def ada_modulation(c_all, w_ada, b_ada):
    L, D, N = w_ada.shape
    R = c_all.shape[0]
    nj = N // D
    return pl.pallas_call(
        _ada_kernel,
        grid=(L, nj),
        in_specs=[
            pl.BlockSpec((R, D), lambda l, j: (0, 0)),
            pl.BlockSpec((None, D, D), lambda l, j: (l, 0, j)),
            pl.BlockSpec((None, 1, D), lambda l, j: (l, 0, j)),
        ],
        out_specs=pl.BlockSpec((None, None, R, D), lambda l, j: (l, j, 0, 0)),
        out_shape=jax.ShapeDtypeStruct((L, nj, R, D), F32),
        compiler_params=_cparams(("parallel", "parallel")),
        name="ada_modulation",
    )(c_all, w_ada, b_ada.reshape(L, 1, N))


class _Mod:
    def __init__(self, arr, rows_per_seq, per_row):
        self.arr = arr
        self.rows_per_seq = rows_per_seq
        self.per_row = per_row

    def spec(self, l, k, tm, ngrid):
        D = self.arr.shape[-1]
        if self.per_row:
            def imap(i, *_):
                return (l, k, i, 0)
            return pl.BlockSpec((None, None, tm, D), imap)
        rps = self.rows_per_seq

        def imap(i, *_):
            return (l, k, (i * tm) // rps, 0, 0)
        return pl.BlockSpec((None, None, None, 1, D), imap)


def _vec_spec(l, s, D):
    return pl.BlockSpec((None, None, 1, D), lambda i, *_: (l, s, 0, 0))


def _ffn_kernel(x_ref, sh_ref, sc_ref, gt_ref, gpre_ref, gpost_ref, wg_ref, wu_ref, wd_ref,
                o_ref, h_ref, acc_ref, *, weight):
    f = pl.program_id(1)

    @pl.when(f == 0)
    def _():
        y = _rms(x_ref[...]) * gpre_ref[...]
        h_ref[...] = (y * (1.0 + sc_ref[...]) + sh_ref[...]).astype(BF16)
        acc_ref[...] = jnp.zeros_like(acc_ref)

    h = h_ref[...]
    g = jnp.dot(h, wg_ref[...], preferred_element_type=F32)
    u = jnp.dot(h, wu_ref[...], preferred_element_type=F32)
    a = (_silu(g) * u).astype(BF16)
    acc_ref[...] += jnp.dot(a, wd_ref[...], preferred_element_type=F32)

    @pl.when(f == pl.num_programs(1) - 1)
    def _():
        y = _rms(acc_ref[...]) * gpost_ref[...]
        o_ref[...] = x_ref[...] + weight * (1.0 + gt_ref[...]) * y


def ffn_block(x, mod, norm_pre, norm_post, wg, wu, wd, l, s, half, tm, tf):
    M, D = x.shape
    F = wg.shape[-1]
    grid = (M // tm, F // tf)
    k0 = 3 * s
    return pl.pallas_call(
        functools.partial(_ffn_kernel, weight=FFN_RESIDUAL),
        grid=grid,
        in_specs=[
            pl.BlockSpec((tm, D), lambda i, f: (i, 0)),
            mod.spec(l, k0 + 0, tm, grid), mod.spec(l, k0 + 1, tm, grid), mod.spec(l, k0 + 2, tm, grid),
            _vec_spec(l, s, D), _vec_spec(l, s, D),
            pl.BlockSpec((None, None, D, tf), lambda i, f: (l, half, 0, f)),
            pl.BlockSpec((None, None, D, tf), lambda i, f: (l, half, 0, f)),
            pl.BlockSpec((None, None, tf, D), lambda i, f: (l, half, f, 0)),
        ],
        out_specs=pl.BlockSpec((tm, D), lambda i, f: (i, 0)),
        out_shape=jax.ShapeDtypeStruct((M, D), F32),
        scratch_shapes=[pltpu.VMEM((tm, D), BF16), pltpu.VMEM((tm, D), F32)],
        compiler_params=_cparams(("parallel", "arbitrary")),
        name="ffn_block",
    )(x, mod.arr, mod.arr, mod.arr, norm_pre, norm_post, wg, wu, wd)


def _inproj_kernel(x_ref, sh_ref, sc_ref, gpre_ref, w_ref, *rest, has_extra):
    if has_extra:
        we_ref, o_ref, oe_ref, h_ref = rest
    else:
        o_ref, h_ref = rest
    j = pl.program_id(1)

    @pl.when(j == 0)
    def _():
        y = _rms(x_ref[...]) * gpre_ref[...]
        h = (y * (1.0 + sc_ref[...]) + sh_ref[...]).astype(BF16)
        h_ref[...] = h
        if has_extra:
            oe_ref[...] = jnp.dot(h, we_ref[...], preferred_element_type=F32)

    o_ref[...] = jnp.dot(h_ref[...], w_ref[...], preferred_element_type=F32)


def inproj_block(x, mod, norm_pre, w, l, s, tm, tn, w_extra=None):
    M, D = x.shape
    N = w.shape[-1]
    i_kind = l // 2
    grid = (M // tm, N // tn)
    k0 = 3 * s
    in_specs = [
        pl.BlockSpec((tm, D), lambda i, j: (i, 0)),
        mod.spec(l, k0 + 0, tm, grid), mod.spec(l, k0 + 1, tm, grid),
        _vec_spec(l, s, D),
        pl.BlockSpec((None, D, tn), lambda i, j: (i_kind, 0, j)),
    ]
    args = [x, mod.arr, mod.arr, norm_pre, w]
    out_specs = [pl.BlockSpec((tm, tn), lambda i, j: (i, j))]
    out_shape = [jax.ShapeDtypeStruct((M, N), F32)]
    if w_extra is not None:
        ne = w_extra.shape[-1]
        in_specs.append(pl.BlockSpec((None, D, ne), lambda i, j: (i_kind, 0, 0)))
        args.append(w_extra)
        out_specs.append(pl.BlockSpec((tm, ne), lambda i, j: (i, 0)))
        out_shape.append(jax.ShapeDtypeStruct((M, ne), F32))
    res = pl.pallas_call(
        functools.partial(_inproj_kernel, has_extra=w_extra is not None),
        grid=grid,
        in_specs=in_specs,
        out_specs=out_specs,
        out_shape=out_shape,
        scratch_shapes=[pltpu.VMEM((tm, D), BF16)],
        compiler_params=_cparams(("parallel", "arbitrary")),
        name="inproj_block",
    )(*args)
    return res if w_extra is not None else res[0]


def _outproj_kernel(x_ref, gt_ref, gpost_ref, a_ref, w_ref, o_ref, acc_ref):
    k = pl.program_id(1)

    @pl.when(k == 0)
    def _():
        acc_ref[...] = jnp.zeros_like(acc_ref)

    acc_ref[...] += jnp.dot(a_ref[...], w_ref[...], preferred_element_type=F32)

    @pl.when(k == pl.num_programs(1) - 1)
    def _():
        y = _rms(acc_ref[...]) * gpost_ref[...]
        o_ref[...] = x_ref[...] + (1.0 + gt_ref[...]) * y


def outproj_block(x, a, mod, norm_post, w, l, s, tm, tk):
    M, D = x.shape
    K = a.shape[-1]
    i_kind = l // 2
    grid = (M // tm, K // tk)
    return pl.pallas_call(
        _outproj_kernel,
        grid=grid,
        in_specs=[
            pl.BlockSpec((tm, D), lambda i, k: (i, 0)),
            mod.spec(l, 3 * s + 2, tm, grid),
            _vec_spec(l, s, D),
            pl.BlockSpec((tm, tk), lambda i, k: (i, k)),
            pl.BlockSpec((None, tk, D), lambda i, k: (i_kind, k, 0)),
        ],
        out_specs=pl.BlockSpec((tm, D), lambda i, k: (i, 0)),
        out_shape=jax.ShapeDtypeStruct((M, D), F32),
        scratch_shapes=[pltpu.VMEM((tm, D), F32)],
        compiler_params=_cparams(("parallel", "arbitrary")),
        name="outproj_block",
    )(x, mod.arr, norm_post, a, w)


CONV_HIST = 32
LANES = 128


def _shifted_taps(ext_ref, w_ref, r0, cs, rc, hist_pad, off, taps, acc):
    n = rc + hist_pad
    win = ext_ref[pl.ds(r0, n), cs]
    for phase in range(8):
        wb = None
        for a8 in range(hist_pad // 8 + 1):
            j = 8 * a8 + phase - off
            if 0 <= j < taps:
                if wb is None:
                    wb = win if phase == 0 else pltpu.roll(win, n - phase, 0)
                acc = acc + w_ref[j:j + 1, cs] * wb[8 * a8:8 * a8 + rc]
    return acc


def _conv_a_kernel(val_ref, gate_ref, st_ref, w_ref, b_ref, lg_ref, lb_ref, a_ref, ns_ref,
                   ext_ref, y_ref, *, tt, rc, taps):
    t = pl.program_id(1)
    hist = taps - 1
    off = CONV_HIST - hist
    ch = val_ref.shape[-1]

    @pl.when(t == 0)
    def _():
        ext_ref[off:CONV_HIST, :] = st_ref[...]

    @pl.when(t > 0)
    def _():
        ext_ref[0:CONV_HIST, :] = ext_ref[tt:tt + CONV_HIST, :]

    ext_ref[CONV_HIST:CONV_HIST + tt, :] = val_ref[...] * jax.nn.sigmoid(gate_ref[...])
    if tt % rc:
        ext_ref[CONV_HIST + tt:, :] = jnp.zeros((ext_ref.shape[0] - CONV_HIST - tt, ch), F32)

    def chunk(r, carry):
        r0 = pl.multiple_of(r * rc, rc)
        for c in range(ch // LANES):
            cs = slice(c * LANES, (c + 1) * LANES)
            acc = _shifted_taps(ext_ref, w_ref, r0, cs, rc, CONV_HIST, off, taps,
                                jnp.broadcast_to(b_ref[:, cs], (rc, LANES)))
            rows = min(rc, tt)
            y_ref[pl.ds(r0, rows), cs] = acc[:rows]
        return carry

    lax.fori_loop(0, pl.cdiv(tt, rc), chunk, 0)

    y = y_ref[...]
    mu = jnp.mean(y, axis=-1, keepdims=True)
    d = y - mu
    var = jnp.mean(d * d, axis=-1, keepdims=True)
    a_ref[...] = _silu(d * lax.rsqrt(var + EPS) * lg_ref[...] + lb_ref[...]).astype(a_ref.dtype)

    @pl.when(t == pl.num_programs(1) - 1)
    def _():
        ns_ref[...] = ext_ref[tt + off:tt + CONV_HIST, :]


def conv_a_block(proj3, conv_state, conv_w, conv_b, ln_g, ln_b, tt, rc):
    B, T, _ = proj3.shape
    taps, CH = conv_w.shape
    row = lambda v: v.reshape(1, CH)
    return pl.pallas_call(
        functools.partial(_conv_a_kernel, tt=tt, rc=rc, taps=taps),
        grid=(B, T // tt),
        in_specs=[
            pl.BlockSpec((None, tt, CH), lambda b, t: (b, t, 0)),
            pl.BlockSpec((None, tt, CH), lambda b, t: (b, t, 1)),
            pl.BlockSpec((None, taps - 1, CH), lambda b, t: (b, 0, 0)),
            pl.BlockSpec((taps, CH), lambda b, t: (0, 0)),
            pl.BlockSpec((1, CH), lambda b, t: (0, 0)),
            pl.BlockSpec((1, CH), lambda b, t: (0, 0)),
            pl.BlockSpec((1, CH), lambda b, t: (0, 0)),
        ],
        out_specs=[
            pl.BlockSpec((None, tt, CH), lambda b, t: (b, t, 0)),
            pl.BlockSpec((None, taps - 1, CH), lambda b, t: (b, 0, 0)),
        ],
        out_shape=[jax.ShapeDtypeStruct((B, T, CH), BF16),
                   jax.ShapeDtypeStruct((B, taps - 1, CH), F32)],
        scratch_shapes=[pltpu.VMEM((CONV_HIST + pl.cdiv(tt, rc) * rc, CH), F32), pltpu.VMEM((tt, CH), F32)],
        compiler_params=_cparams(("parallel", "arbitrary")),
        name="conv_a_block",
    )(proj3, proj3, conv_state, conv_w, row(conv_b), row(ln_g), row(ln_b))


def _log_sigmoid_pair(z):
    ls = jnp.minimum(z, 0.0) - jnp.log1p(jnp.exp(-jnp.abs(z)))
    return ls, ls - z


def _suffix_sum(lk, u_ref):
    hi = lk.astype(BF16)
    lo = (lk - hi.astype(F32)).astype(BF16)
    u = u_ref[...]
    return jnp.dot(hi, u, preferred_element_type=F32) + jnp.dot(lo, u, preferred_element_type=F32)


def _sb_prompt_kernel(bias_ref, q_ref, k_ref, v_ref, u_ref, o_ref, k16_ref, v16_ref, *, tq, tk, scale):
    h = pl.program_id(1)
    i = pl.program_id(2)

    @pl.when(i == 0)
    def _():
        k16_ref[...] = k_ref[...].astype(BF16)
        v16_ref[...] = v_ref[...].astype(BF16)

    bias = bias_ref[h]
    q16 = q_ref[...].astype(BF16)
    nd = tq // tk

    def block(j0, m, acc, diag_idx):
        kb = k16_ref[pl.ds(j0, tk), :]
        vb = v16_ref[pl.ds(j0, tk), :]
        z = lax.dot_general(q16, kb, (((1,), (1,)), ((), ())), preferred_element_type=F32) * scale + bias
        ls, lk = _log_sigmoid_pair(z)
        if diag_idx is not None:
            qpos = lax.broadcasted_iota(jnp.int32, (tq, tk), 0)
            kpos = lax.broadcasted_iota(jnp.int32, (tq, tk), 1) + diag_idx * tk
            vis = kpos < qpos
            ls = jnp.where(vis, ls, -jnp.inf)
            lk = jnp.where(vis, lk, 0.0)
        later = _suffix_sum(lk, u_ref)
        w = jnp.exp(ls + later + m)
        acc = acc + jnp.dot(w.astype(BF16), vb, preferred_element_type=F32)
        m = m + later[:, 0:1] + lk[:, 0:1]
        return m, acc

    m = jnp.zeros((tq, 1), F32)
    acc = jnp.zeros((tq, o_ref.shape[-1]), F32)
    base = pl.multiple_of(i * tq, tq)
    for d in reversed(range(nd)):
        m, acc = block(base + d * tk, m, acc, d)

    def body(jj, carry):
        j0 = pl.multiple_of(base - (jj + 1) * tk, tk)
        return block(j0, carry[0], carry[1], None)

    m, acc = lax.fori_loop(0, i * nd, body, (m, acc))
    o_ref[...] = acc.astype(o_ref.dtype)


def _suffix_matrix(n):
    r = lax.broadcasted_iota(jnp.int32, (n, n), 0)
    c = lax.broadcasted_iota(jnp.int32, (n, n), 1)
    return (r > c).astype(BF16)


def sb_attention_prompt(proj3, sb_bias, tq, tk, heads, q_col, k_col, v_col):
    B, T, _ = proj3.shape
    Dh = LANES
    return pl.pallas_call(
        functools.partial(_sb_prompt_kernel, tq=tq, tk=tk, scale=Dh ** -0.5),
        grid=(B, heads, T // tq),
        in_specs=[
            pl.BlockSpec(memory_space=pltpu.SMEM),
            pl.BlockSpec((None, tq, Dh), lambda b, h, i: (b, i, q_col + h)),
            pl.BlockSpec((None, T, Dh), lambda b, h, i: (b, 0, k_col + h)),
            pl.BlockSpec((None, T, Dh), lambda b, h, i: (b, 0, v_col + h)),
            pl.BlockSpec((tk, tk), lambda b, h, i: (0, 0)),
        ],
        out_specs=pl.BlockSpec((None, tq, Dh), lambda b, h, i: (b, i, h)),
        out_shape=jax.ShapeDtypeStruct((B, T, heads * Dh), BF16),
        scratch_shapes=[pltpu.VMEM((T, Dh), BF16), pltpu.VMEM((T, Dh), BF16)],
        compiler_params=_cparams(("parallel", "parallel", "arbitrary")),
        name="sb_attention_prompt",
    )(sb_bias, proj3, proj3, proj3, _suffix_matrix(tk))


def _sb_sample_kernel(tbl_ref, q_ref, knew_ref, vnew_ref, bias_ref, u_ref, *rest, pp, page, heads, scale):
    page_refs = rest[:pp]
    o_ref, m_ref = rest[pp], rest[pp + 1]
    g = pl.program_id(1)
    R = q_ref.shape[0]
    q16 = q_ref[...].astype(BF16)
    bias = bias_ref[...]
    rhead = lax.broadcasted_iota(jnp.int32, (R, LANES), 0) % heads
    rtime = lax.broadcasted_iota(jnp.int32, (R, LANES), 0) // heads
    ccol = lax.broadcasted_iota(jnp.int32, (R, LANES), 1)
    same_head = (ccol % heads) == rhead

    def sweep(k2d, v2d, m, acc, vis_of_chunk):
        n = k2d.shape[0]
        z = lax.dot_general(q16, k2d.astype(BF16), (((1,), (1,)), ((), ())),
                            preferred_element_type=F32) * scale + bias
        ls, lk = _log_sigmoid_pair(z)
        nch = n // LANES
        lks, lss = [], []
        for c in range(nch):
            vis = vis_of_chunk(c)
            lss.append(jnp.where(vis, ls[:, c * LANES:(c + 1) * LANES], -jnp.inf))
            lks.append(jnp.where(vis, lk[:, c * LANES:(c + 1) * LANES], 0.0))
        later = _suffix_sum(jnp.concatenate(lks, axis=0), u_ref)
        ws = [None] * nch
        for c in reversed(range(nch)):
            lat = later[c * R:(c + 1) * R]
            ws[c] = jnp.exp(lss[c] + lat + m).astype(BF16)
            m = m + lat[:, 0:1] + lks[c][:, 0:1]
        acc = acc + jnp.dot(jnp.concatenate(ws, axis=1), v2d.astype(BF16), preferred_element_type=F32)
        return m, acc

    @pl.when(g == 0)
    def _():
        def vis_new(c):
            return same_head & ((ccol // heads) < rtime)
        m, acc = sweep(knew_ref[...], vnew_ref[...], jnp.zeros((R, 1), F32),
                       jnp.zeros(o_ref.shape, F32), vis_new)
        m_ref[...] = m
        o_ref[...] = acc

    m, acc = m_ref[...], o_ref[...]
    for u in range(pp):
        pr = page_refs[u]
        k2d = pr[:, 0].reshape(page * heads, LANES)
        v2d = pr[:, 1].reshape(page * heads, LANES)
        m, acc = sweep(k2d, v2d, m, acc, lambda c: same_head)
    m_ref[...] = m
    o_ref[...] = acc


def sb_attention_sample(q, k_new, v_new, cache_kv, layer_idx, page_table, sb_bias, pp):
    B, R, Dh = q.shape
    _, _, page, _, heads, _ = cache_kv.shape
    n_pages = page_table.shape[1]
    bias_col = jnp.tile(sb_bias, R // heads).reshape(R, 1)

    def page_spec(u):
        def imap(b, g, tbl):
            return (layer_idx, tbl[b, n_pages - 1 - (g * pp + u)], 0, 0, 0, 0)
        return pl.BlockSpec((None, None, page, 2, heads, Dh), imap)

    grid_spec = pltpu.PrefetchScalarGridSpec(
        num_scalar_prefetch=1,
        grid=(B, n_pages // pp),
        in_specs=[
            pl.BlockSpec((None, R, Dh), lambda b, g, tbl: (b, 0, 0)),
            pl.BlockSpec((None, LANES, Dh), lambda b, g, tbl: (b, 0, 0)),
            pl.BlockSpec((None, LANES, Dh), lambda b, g, tbl: (b, 0, 0)),
            pl.BlockSpec((R, 1), lambda b, g, tbl: (0, 0)),
            pl.BlockSpec((LANES, LANES), lambda b, g, tbl: (0, 0)),
        ] + [page_spec(u) for u in range(pp)],
        out_specs=pl.BlockSpec((None, R, Dh), lambda b, g, tbl: (b, 0, 0)),
        scratch_shapes=[pltpu.VMEM((R, 1), F32)],
    )
    return pl.pallas_call(
        functools.partial(_sb_sample_kernel, pp=pp, page=page, heads=heads, scale=Dh ** -0.5),
        grid_spec=grid_spec,
        out_shape=jax.ShapeDtypeStruct((B, R, Dh), F32),
        compiler_params=_cparams(("parallel", "arbitrary")),
        name="sb_attention_sample",
    )(page_table, q, k_new, v_new, bias_col, _suffix_matrix(LANES), *([cache_kv] * pp))


GDN_HIST = 8


def _gdn_prep_kernel(x_ref, st_ref, w_ref, o_ref, ns_ref, ext_ref, *, tt, rc, taps, n_q, n_k, scale):
    j = pl.program_id(1)
    t = pl.program_id(2)
    hist = taps - 1
    off = GDN_HIST - hist
    tc = x_ref.shape[-1]

    @pl.when(t == 0)
    def _():
        ext_ref[off:GDN_HIST, :] = st_ref[...]

    @pl.when(t > 0)
    def _():
        ext_ref[0:GDN_HIST, :] = ext_ref[tt:tt + GDN_HIST, :]

    ext_ref[GDN_HIST:GDN_HIST + tt, :] = x_ref[...]
    if tt % rc:
        ext_ref[GDN_HIST + tt:, :] = jnp.zeros((ext_ref.shape[0] - GDN_HIST - tt, tc), F32)

    rows = min(rc, tt)

    def run(normalise, post_scale):
        def chunk(r, carry):
            r0 = pl.multiple_of(r * rc, rc)
            for c in range(tc // LANES):
                cs = slice(c * LANES, (c + 1) * LANES)
                y = _silu(_shifted_taps(ext_ref, w_ref, r0, cs, rc, GDN_HIST, off, taps,
                                        jnp.zeros((rc, LANES), F32)))
                if normalise:
                    y = y * lax.rsqrt(jnp.sum(y * y, axis=-1, keepdims=True) + EPS)
                    if post_scale != 1.0:
                        y = y * post_scale
                o_ref[pl.ds(r0, rows), cs] = y[:rows]
            return carry
        lax.fori_loop(0, pl.cdiv(tt, rc), chunk, 0)

    @pl.when(j < n_q)
    def _():
        run(True, scale)

    @pl.when((j >= n_q) & (j < n_q + n_k))
    def _():
        run(True, 1.0)

    @pl.when(j >= n_q + n_k)
    def _():
        run(False, 1.0)

    @pl.when(t == pl.num_programs(2) - 1)
    def _():
        ns_ref[...] = ext_ref[tt + off:tt + GDN_HIST, :]


def gdn_prep_block(proj3, conv_state, conv_w, kw, tt, rc, tc):
    B, T, _ = proj3.shape
    taps, QKV = conv_w.shape
    return pl.pallas_call(
        functools.partial(_gdn_prep_kernel, tt=tt, rc=rc, taps=taps, n_q=kw // tc, n_k=kw // tc,
                          scale=LANES ** -0.5),
        grid=(B, QKV // tc, T // tt),
        in_specs=[
            pl.BlockSpec((None, tt, tc), lambda b, j, t: (b, t, j)),
            pl.BlockSpec((None, taps - 1, tc), lambda b, j, t: (b, 0, j)),
            pl.BlockSpec((taps, tc), lambda b, j, t: (0, j)),
        ],
        out_specs=[
            pl.BlockSpec((None, tt, tc), lambda b, j, t: (b, t, j)),
            pl.BlockSpec((None, taps - 1, tc), lambda b, j, t: (b, 0, j)),
        ],
        out_shape=[jax.ShapeDtypeStruct((B, T, QKV), F32),
                   jax.ShapeDtypeStruct((B, taps - 1, QKV), F32)],
        scratch_shapes=[pltpu.VMEM((GDN_HIST + pl.cdiv(tt, rc) * rc, tc), F32)],
        compiler_params=_cparams(("parallel", "parallel", "arbitrary")),
        name="gdn_prep_block",
    )(proj3, conv_state, conv_w)


def _split3(x):
    hi = x.astype(BF16)
    r = x - hi.astype(F32)
    mid = r.astype(BF16)
    lo = (r - mid.astype(F32)).astype(BF16)
    return hi, mid, lo


def _gdn_gates_kernel(ba_ref, alog_ref, dtb_ref, tri_ref, o_ref, *, nh, seq_rows, valid_rows):
    ba = ba_ref[...]
    R = ba.shape[0]
    beta = jax.nn.sigmoid(ba[:, :nh])
    x = ba[:, nh:] + dtb_ref[...]
    softplus = jnp.maximum(x, 0.0) + jnp.log1p(jnp.exp(-jnp.abs(x)))
    g = -jnp.exp(alog_ref[...]) * softplus
    if valid_rows < seq_rows:
        row = lax.broadcasted_iota(jnp.int32, (R, nh), 0) % seq_rows
        beta = jnp.where(row < valid_rows, beta, 0.0)
        g = jnp.where(row < valid_rows, g, 0.0)
    tri = tri_ref[...]
    gc = sum(jnp.dot(tri, p, preferred_element_type=F32) for p in _split3(g))
    o_ref[...] = jnp.concatenate([gc, beta, jnp.zeros((R, LANES - 2 * nh), F32)], axis=1)


def gdn_gates(ba, a_log, dt_bias, chunk, seq_rows, valid_rows, tr):
    M, two_nh = ba.shape
    nh = two_nh // 2
    r = lax.broadcasted_iota(jnp.int32, (tr, tr), 0)
    c = lax.broadcasted_iota(jnp.int32, (tr, tr), 1)
    tri = ((r // chunk == c // chunk) & (r >= c)).astype(BF16)
    return pl.pallas_call(
        functools.partial(_gdn_gates_kernel, nh=nh, seq_rows=seq_rows, valid_rows=valid_rows),
        grid=(M // tr,),
        in_specs=[
            pl.BlockSpec((tr, two_nh), lambda i: (i, 0)),
            pl.BlockSpec((1, nh), lambda i: (0, 0)),
            pl.BlockSpec((1, nh), lambda i: (0, 0)),
            pl.BlockSpec((tr, tr), lambda i: (0, 0)),
        ],
        out_specs=pl.BlockSpec((tr, LANES), lambda i: (i, 0)),
        out_shape=jax.ShapeDtypeStruct((M, LANES), F32),
        compiler_params=_cparams(("parallel",)),
        name="gdn_gates",
    )(ba, a_log.reshape(1, nh), dt_bias.reshape(1, nh), tri)


GDN_INV_PASSES = 3


def _mm(a, b, passes=1):
    if passes == 1:
        return jnp.dot(a.astype(BF16), b.astype(BF16), preferred_element_type=F32)
    ah = a.astype(BF16)
    al = (a - ah.astype(F32)).astype(BF16)
    bh = b.astype(BF16)
    bl = (b - bh.astype(F32)).astype(BF16)
    return (jnp.dot(ah, bh, preferred_element_type=F32) + jnp.dot(ah, bl, preferred_element_type=F32)
            + jnp.dot(al, bh, preferred_element_type=F32))


def _unit_lower_inverse(a, eye, n):
    x = eye - a
    p = a
    k = 2
    while k < n:
        p = _mm(p, p, GDN_INV_PASSES)
        x = x + _mm(x, p, GDN_INV_PASSES)
        k *= 2
    return x


def _gdn_chunk_kernel(q_ref, k_ref, v_ref, z_ref, gc_ref, bt_ref, s0_ref, ng_ref, o_ref, so_ref, s_ref,
                      *, nc, hk, C):
    t = pl.program_id(2)
    Dh = LANES

    @pl.when(t == 0)
    def _():
        s_ref[...] = s0_ref[...]

    ii = lax.broadcasted_iota(jnp.int32, (C, C), 0)
    jj = lax.broadcasted_iota(jnp.int32, (C, C), 1)
    causal = ii >= jj
    strict = ii > jj
    diag = ii == jj
    eye = diag.astype(F32)
    nt = (((1,), (1,)), ((), ()))
    tn = (((0,), (0,)), ((), ()))

    for kk in range(hk):
        S = [s_ref[2 * kk + e] for e in range(2)]
        ks = slice(kk * Dh, (kk + 1) * Dh)
        for c in range(nc):
            rs = slice(c * C, (c + 1) * C)
            kc = k_ref[rs, ks]
            qc = q_ref[rs, ks]
            k16 = kc.astype(BF16)
            gkk = lax.dot_general(k16, k16, nt, preferred_element_type=F32)
            gqk = lax.dot_general(qc.astype(BF16), k16, nt, preferred_element_type=F32)
            for e in range(2):
                hv = 2 * kk + e
                vs = slice(hv * Dh, (hv + 1) * Dh)
                grow = gc_ref[hv, :, rs]
                brow = bt_ref[hv, :, rs]
                gcol = jnp.sum(jnp.where(diag, grow, 0.0), axis=1, keepdims=True)
                bcol = jnp.sum(jnp.where(diag, brow, 0.0), axis=1, keepdims=True)
                decay = jnp.exp(jnp.where(causal, gcol - grow, -jnp.inf))
                a = jnp.where(strict, gkk * bcol * decay, 0.0)
                tinv = _unit_lower_inverse(a, eye, C)
                egc = jnp.exp(gcol)
                glast = grow[:, C - 1:C]
                rhs = jnp.concatenate([v_ref[rs, vs] * bcol, kc * (bcol * egc)], axis=1)
                sol = _mm(tinv, rhs)
                u, w = sol[:, :Dh], sol[:, Dh:]
                attn = gqk * decay
                qdec = qc * egc
                kdec = kc * jnp.exp(glast - gcol)
                s_cur = S[e]
                v_new = u - _mm(w, s_cur)
                o = _mm(qdec, s_cur) + _mm(attn, v_new)
                S[e] = s_cur * jnp.exp(glast) + lax.dot_general(
                    kdec.astype(BF16), v_new.astype(BF16), tn, preferred_element_type=F32)
                o = _rms(o) * ng_ref[...] * _silu(z_ref[rs, vs])
                o_ref[rs, vs] = o.astype(o_ref.dtype)
        for e in range(2):
            s_ref[2 * kk + e] = S[e]

    @pl.when(t == pl.num_programs(2) - 1)
    def _():
        so_ref[...] = s_ref[...]


def gdn_chunk_block(qkv, proj3, gates_t, s0, norm_g, kw, z_col0, chunk, nc, hk):
    B, T, QKV = qkv.shape
    Dh = LANES
    nh = s0.shape[1]
    kh = kw // Dh
    tcs = nc * chunk
    qw, vw = hk * Dh, 2 * hk * Dh
    return pl.pallas_call(
        functools.partial(_gdn_chunk_kernel, nc=nc, hk=hk, C=chunk),
        grid=(B, kh // hk, T // tcs),
        in_specs=[
            pl.BlockSpec((None, tcs, qw), lambda b, g, t: (b, t, g)),
            pl.BlockSpec((None, tcs, qw), lambda b, g, t: (b, t, kw // qw + g)),
            pl.BlockSpec((None, tcs, vw), lambda b, g, t: (b, t, 2 * kw // vw + g)),
            pl.BlockSpec((None, tcs, vw), lambda b, g, t: (b, t, z_col0 // vw + g)),
            pl.BlockSpec((None, 2 * hk, 1, tcs), lambda b, g, t: (b, g, 0, t)),
            pl.BlockSpec((None, 2 * hk, 1, tcs), lambda b, g, t: (b, nh // (2 * hk) + g, 0, t)),
            pl.BlockSpec((None, 2 * hk, Dh, Dh), lambda b, g, t: (b, g, 0, 0)),
            pl.BlockSpec((1, Dh), lambda b, g, t: (0, 0)),
        ],
        out_specs=[
            pl.BlockSpec((None, tcs, vw), lambda b, g, t: (b, t, g)),
            pl.BlockSpec((None, 2 * hk, Dh, Dh), lambda b, g, t: (b, g, 0, 0)),
        ],
        out_shape=[jax.ShapeDtypeStruct((B, T, nh * Dh), BF16),
                   jax.ShapeDtypeStruct((B, nh, Dh, Dh), F32)],
        scratch_shapes=[pltpu.VMEM((2 * hk, Dh, Dh), F32)],
        compiler_params=_cparams(("parallel", "parallel", "arbitrary")),
        name="gdn_chunk_block",
    )(qkv, qkv, qkv, proj3, gates_t, gates_t, s0, norm_g.reshape(1, Dh))


GDN_CHUNK = 64


class _Cfg:
    def __init__(self, prompt):
        self.prompt = prompt
        if prompt:
            self.tm, self.conv_tt, self.conv_rc = 512, 256, 64
            self.gdn_tt, self.gdn_rc, self.gdn_nc = 256, 64, 4
        else:
            self.tm, self.conv_tt, self.conv_rc = None, None, 8
            self.gdn_tt, self.gdn_rc, self.gdn_nc = None, 8, 1


def _conv_sb_mixer(proj3, i, cfg, conv_state, cache_kv, table, P):
    B, T, _ = proj3.shape
    ch = P['conv_a_w'].shape[-1]
    heads, Dh = cache_kv.shape[-2], cache_kv.shape[-1]
    sbw = heads * Dh
    qb, kb, vb = 2 * ch // Dh, (2 * ch + sbw) // Dh, (2 * ch + 2 * sbw) // Dh
    a16, cs = conv_a_block(proj3, conv_state, P['conv_a_w'][i], P['conv_a_b'][i], P['ln_a_g'][i],
                           P['ln_a_b'][i], cfg.conv_tt or T, cfg.conv_rc)
    kv = proj3[:, :, 2 * ch + sbw:].reshape(B, T, 2, heads, Dh)
    if cfg.prompt:
        o16 = sb_attention_prompt(proj3, P['sb_bias'][i], 512, 256, heads, qb, kb, vb)
    else:
        rows = T * heads
        q = proj3[:, :, 2 * ch:2 * ch + sbw].reshape(B, rows, Dh)
        pad = lambda x: jnp.pad(x.reshape(B, rows, Dh), ((0, 0), (0, LANES - rows), (0, 0)))
        o = sb_attention_sample(q, pad(kv[:, :, 0]), pad(kv[:, :, 1]), cache_kv, i, table, P['sb_bias'][i], 4)
        o16 = o.reshape(B, T, sbw).astype(BF16)
    cat = jnp.concatenate([a16, o16], axis=-1)
    return cat.reshape(B * T, ch + sbw), cs, kv


def _gdn_mixer(proj3, ba, i, cfg, conv_state, s0, P):
    B, T, _ = proj3.shape
    nh = s0.shape[1]
    kw = P['w_out_gdn'].shape[1] // 2
    qkv_w = P['conv_c_w'].shape[-1]
    qkv, cs = gdn_prep_block(proj3, conv_state, P['conv_c_w'][i], kw, cfg.gdn_tt or T, cfg.gdn_rc, 1024)
    if T % GDN_CHUNK:
        tp = -(-T // GDN_CHUNK) * GDN_CHUNK
        padt = lambda x: jnp.pad(x, ((0, 0), (0, tp - T), (0, 0)))
        qkv, proj3, ba = padt(qkv), padt(proj3), padt(ba)
    else:
        tp = T
    gates = gdn_gates(ba.reshape(B * tp, 2 * nh), P['a_log'][i], P['dt_bias'][i], GDN_CHUNK, tp, T,
                      min(512, B * tp))
    gates_t = gates.reshape(B, tp, LANES)[:, :, :2 * nh].transpose(0, 2, 1).reshape(B, 2 * nh, 1, tp)
    og, s = gdn_chunk_block(qkv, proj3, gates_t, s0, P['gdn_norm_g'][i], kw, qkv_w, GDN_CHUNK, cfg.gdn_nc, 1)
    return og[:, :T].reshape(B * T, -1), cs, s


def _trunk(x, mod, B, T, cfg, cache_kv, table, conv_a_state, conv_c_state, gdn_state, P):
    D = x.shape[-1]
    depth = P['norm_pre'].shape[0]
    tm = cfg.tm or B * T
    tf = 512
    kv_rows, conv_a_new, conv_c_new, gdn_new = [], [], [], []
    for l in range(depth):
        i = l // 2
        x = ffn_block(x, mod, P['norm_pre'], P['norm_post'], P['wg'], P['wu'], P['wd'], l, 0, 0, tm, tf)
        if l % 2 == 0:
            proj = inproj_block(x, mod, P['norm_pre'], P['w_in_ab'], l, 1, tm, 1024)
            cat, cs, kv = _conv_sb_mixer(proj.reshape(B, T, -1), i, cfg, conv_a_state[i], cache_kv, table, P)
            conv_a_new.append(cs)
            kv_rows.append(kv)
            x = outproj_block(x, cat, mod, P['norm_post'], P['w_out_ab'], l, 1, tm, 1024)
        else:
            proj, ba = inproj_block(x, mod, P['norm_pre'], P['w_in_gdn'], l, 1, tm, 1024, w_extra=P['w_in_gdn_ba'])
            og, cs, s = _gdn_mixer(proj.reshape(B, T, -1), ba.reshape(B, T, -1), i, cfg, conv_c_state[i],
                                   gdn_state[i], P)
            conv_c_new.append(cs)
            gdn_new.append(s)
            x = outproj_block(x, og, mod, P['norm_post'], P['w_out_gdn'], l, 1, tm, 1024)
        x = ffn_block(x, mod, P['norm_pre'], P['norm_post'], P['wg'], P['wu'], P['wd'], l, 2, 1, tm, tf)
    return (x.reshape(B, T, D), jnp.stack(kv_rows), jnp.stack(conv_a_new), jnp.stack(conv_c_new),
            jnp.stack(gdn_new))


def kernel(x_prompt, x_sample, cache_kv_sb, state_conv_a, state_conv_c, state_gdn, page_table,
           c_prompt, c_sample, w_ada, b_ada, norm_pre, norm_post, w_ffn_gate, w_ffn_up, w_ffn_down,
           w_in_ab, conv_a_w, conv_a_b, ln_a_g, ln_a_b, w_out_ab, sb_bias,
           w_in_gdn, conv_c_w, a_log, dt_bias, gdn_norm_g, w_out_gdn):
    BP, T, D = x_prompt.shape
    BS, TS, _ = x_sample.shape
    L = w_ada.shape[0]
    n_sb, n_gdn = w_in_ab.shape[0], w_in_gdn.shape[0]
    dt = x_prompt.dtype

    R = -(-(BP + BS) // 8) * 8
    c_all = jnp.concatenate([c_prompt, c_sample, jnp.zeros((R - BP - BS, D), dt)], axis=0)
    mod_all = ada_modulation(c_all, w_ada, b_ada)
    mod_p = _Mod(mod_all[:, :, :BP].reshape(L, 9, BP, 1, D), T, False)
    mod_s = _Mod(jnp.repeat(mod_all[:, :, BP:BP + BS], TS, axis=2), TS, True)

    qkvz = conv_c_w.shape[-1] + w_out_gdn.shape[1]
    P = dict(
        norm_pre=norm_pre.reshape(L, 3, 1, D), norm_post=norm_post.reshape(L, 3, 1, D),
        wg=w_ffn_gate.astype(BF16), wu=w_ffn_up.astype(BF16), wd=w_ffn_down.astype(BF16),
        w_in_ab=w_in_ab.astype(BF16), w_out_ab=w_out_ab.astype(BF16),
        w_in_gdn=w_in_gdn[:, :, :qkvz].astype(BF16), w_in_gdn_ba=w_in_gdn[:, :, qkvz:].astype(BF16),
        w_out_gdn=w_out_gdn.astype(BF16),
        conv_a_w=conv_a_w, conv_a_b=conv_a_b, ln_a_g=ln_a_g, ln_a_b=ln_a_b, sb_bias=sb_bias,
        conv_c_w=conv_c_w, a_log=a_log, dt_bias=dt_bias, gdn_norm_g=gdn_norm_g,
    )

    y_p, kv_p, ca_p, cc_p, g_p = _trunk(
        x_prompt.reshape(BP * T, D), mod_p, BP, T, _Cfg(True), cache_kv_sb, None,
        jnp.zeros((n_sb, BP) + state_conv_a.shape[2:], dt), jnp.zeros((n_gdn, BP) + state_conv_c.shape[2:], dt),
        jnp.zeros((n_gdn, BP) + state_gdn.shape[2:], dt), P)
    y_s, kv_s, ca_s, cc_s, g_s = _trunk(
        x_sample.reshape(BS * TS, D), mod_s, BS, TS, _Cfg(False), cache_kv_sb, page_table,
        state_conv_a, state_conv_c, state_gdn, P)
    return (y_p, y_s, kv_p, ca_p, cc_p, g_p, kv_s, ca_s, cc_s, g_s)
```

```python
import functools
import math

import jax
import jax.numpy as jnp
from jax import lax
from jax.experimental import pallas as pl
from jax.experimental.pallas import tpu as pltpu

F32 = jnp.float32
BF16 = jnp.bfloat16
EPS = 1e-6
FFN_RESIDUAL = 0.5
VMEM_LIMIT = 56 * 1024 * 1024


def _cparams(sem):
    return pltpu.CompilerParams(dimension_semantics=sem, vmem_limit_bytes=VMEM_LIMIT)


def _silu(x):
    return x * jax.nn.sigmoid(x)


def _rms(x):
    return x * lax.rsqrt(jnp.mean(x * x, axis=-1, keepdims=True) + EPS)


def _ada_kernel(c_ref, w_ref, b_ref, o_ref):
    s = _silu(c_ref[...]).astype(BF16)
    o_ref[...] = jnp.dot(s, w_ref[...].astype(BF16), preferred_element_type=F32) + b_ref[...]


def ada_modulation(c_all, w_ada, b_ada):
    L, D, N = w_ada.shape
    R = c_all.shape[0]
    nj = N // D
    return pl.pallas_call(
        _ada_kernel,
        grid=(L, nj),
        in_specs=[
            pl.BlockSpec((R, D), lambda l, j: (0, 0)),
            pl.BlockSpec((None, D, D), lambda l, j: (l, 0, j)),
            pl.BlockSpec((None, 1, D), lambda l, j: (l, 0, j)),
        ],
        out_specs=pl.BlockSpec((None, None, R, D), lambda l, j: (l, j, 0, 0)),
        out_shape=jax.ShapeDtypeStruct((L, nj, R, D), F32),
        compiler_params=_cparams(("parallel", "parallel")),
        name="ada_modulation",
    )(c_all, w_ada, b_ada.reshape(L, 1, N))


class _Mod:
    def __init__(self, arr, rows_per_seq, per_row):
        self.arr = arr
        self.rows_per_seq = rows_per_seq
        self.per_row = per_row

    def spec(self, l, k, tm, ngrid):
        D = self.arr.shape[-1]
        if self.per_row:
            def imap(i, *_):
                return (l, k, i, 0)
            return pl.BlockSpec((None, None, tm, D), imap)
        rps = self.rows_per_seq

        def imap(i, *_):
            return (l, k, (i * tm) // rps, 0, 0)
        return pl.BlockSpec((None, None, None, 1, D), imap)


def _vec_spec(l, s, D):
    return pl.BlockSpec((None, None, 1, D), lambda i, *_: (l, s, 0, 0))


def _ffn_kernel(x_ref, sh_ref, sc_ref, gt_ref, gpre_ref, gpost_ref, wg_ref, wu_ref, wd_ref,
                o_ref, h_ref, acc_ref, *, weight):
    f = pl.program_id(1)

    @pl.when(f == 0)
    def _():
        y = _rms(x_ref[...]) * gpre_ref[...]
        h_ref[...] = (y * (1.0 + sc_ref[...]) + sh_ref[...]).astype(BF16)
        acc_ref[...] = jnp.zeros_like(acc_ref)

    h = h_ref[...]
    g = jnp.dot(h, wg_ref[...], preferred_element_type=F32)
    u = jnp.dot(h, wu_ref[...], preferred_element_type=F32)
    a = (_silu(g) * u).astype(BF16)
    acc_ref[...] += jnp.dot(a, wd_ref[...], preferred_element_type=F32)

    @pl.when(f == pl.num_programs(1) - 1)
    def _():
        y = _rms(acc_ref[...]) * gpost_ref[...]
        o_ref[...] = x_ref[...] + weight * (1.0 + gt_ref[...]) * y


def ffn_block(x, mod, norm_pre, norm_post, wg, wu, wd, l, s, half, tm, tf):
    M, D = x.shape
    F = wg.shape[-1]
    grid = (M // tm, F // tf)
    k0 = 3 * s
    return pl.pallas_call(
        functools.partial(_ffn_kernel, weight=FFN_RESIDUAL),
        grid=grid,
        in_specs=[
            pl.BlockSpec((tm, D), lambda i, f: (i, 0)),
            mod.spec(l, k0 + 0, tm, grid), mod.spec(l, k0 + 1, tm, grid), mod.spec(l, k0 + 2, tm, grid),
            _vec_spec(l, s, D), _vec_spec(l, s, D),
            pl.BlockSpec((None, None, D, tf), lambda i, f: (l, half, 0, f)),
            pl.BlockSpec((None, None, D, tf), lambda i, f: (l, half, 0, f)),
            pl.BlockSpec((None, None, tf, D), lambda i, f: (l, half, f, 0)),
        ],
        out_specs=pl.BlockSpec((tm, D), lambda i, f: (i, 0)),
        out_shape=jax.ShapeDtypeStruct((M, D), F32),
        scratch_shapes=[pltpu.VMEM((tm, D), BF16), pltpu.VMEM((tm, D), F32)],
        compiler_params=_cparams(("parallel", "arbitrary")),
        name="ffn_block",
    )(x, mod.arr, mod.arr, mod.arr, norm_pre, norm_post, wg, wu, wd)


def _inproj_kernel(x_ref, sh_ref, sc_ref, gpre_ref, w_ref, *rest, has_extra):
    if has_extra:
        we_ref, o_ref, oe_ref, h_ref = rest
    else:
        o_ref, h_ref = rest
    j = pl.program_id(1)

    @pl.when(j == 0)
    def _():
        y = _rms(x_ref[...]) * gpre_ref[...]
        h = (y * (1.0 + sc_ref[...]) + sh_ref[...]).astype(BF16)
        h_ref[...] = h
        if has_extra:
            oe_ref[...] = jnp.dot(h, we_ref[...], preferred_element_type=F32)

    o_ref[...] = jnp.dot(h_ref[...], w_ref[...], preferred_element_type=F32)


def inproj_block(x, mod, norm_pre, w, l, s, tm, tn, w_extra=None):
    M, D = x.shape
    N = w.shape[-1] // tn * tn
    i_kind = l // 2
    grid = (M // tm, N // tn)
    k0 = 3 * s
    in_specs = [
        pl.BlockSpec((tm, D), lambda i, j: (i, 0)),
        mod.spec(l, k0 + 0, tm, grid), mod.spec(l, k0 + 1, tm, grid),
        _vec_spec(l, s, D),
        pl.BlockSpec((None, D, tn), lambda i, j: (i_kind, 0, j)),
    ]
    args = [x, mod.arr, mod.arr, norm_pre, w]
    out_specs = [pl.BlockSpec((tm, tn), lambda i, j: (i, j))]
    out_shape = [jax.ShapeDtypeStruct((M, N), F32)]
    if w_extra is not None:
        ne = w_extra.shape[-1]
        in_specs.append(pl.BlockSpec((None, D, ne), lambda i, j: (i_kind, 0, 0)))
        args.append(w_extra)
        out_specs.append(pl.BlockSpec((tm, ne), lambda i, j: (i, 0)))
        out_shape.append(jax.ShapeDtypeStruct((M, ne), F32))
    res = pl.pallas_call(
        functools.partial(_inproj_kernel, has_extra=w_extra is not None),
        grid=grid,
        in_specs=in_specs,
        out_specs=out_specs,
        out_shape=out_shape,
        scratch_shapes=[pltpu.VMEM((tm, D), BF16)],
        compiler_params=_cparams(("parallel", "arbitrary")),
        name="inproj_block",
    )(*args)
    return res if w_extra is not None else res[0]


def _outproj_kernel(x_ref, gt_ref, gpost_ref, a_ref, w_ref, o_ref, acc_ref):
    k = pl.program_id(1)

    @pl.when(k == 0)
    def _():
        acc_ref[...] = jnp.zeros_like(acc_ref)

    acc_ref[...] += jnp.dot(a_ref[...], w_ref[...], preferred_element_type=F32)

    @pl.when(k == pl.num_programs(1) - 1)
    def _():
        y = _rms(acc_ref[...]) * gpost_ref[...]
        o_ref[...] = x_ref[...] + (1.0 + gt_ref[...]) * y


def outproj_block(x, a, mod, norm_post, w, l, s, tm, tk):
    M, D = x.shape
    K = a.shape[-1]
    i_kind = l // 2
    grid = (M // tm, K // tk)
    return pl.pallas_call(
        _outproj_kernel,
        grid=grid,
        in_specs=[
            pl.BlockSpec((tm, D), lambda i, k: (i, 0)),
            mod.spec(l, 3 * s + 2, tm, grid),
            _vec_spec(l, s, D),
            pl.BlockSpec((tm, tk), lambda i, k: (i, k)),
            pl.BlockSpec((None, tk, D), lambda i, k: (i_kind, k, 0)),
        ],
        out_specs=pl.BlockSpec((tm, D), lambda i, k: (i, 0)),
        out_shape=jax.ShapeDtypeStruct((M, D), F32),
        scratch_shapes=[pltpu.VMEM((tm, D), F32)],
        compiler_params=_cparams(("parallel", "arbitrary")),
        name="outproj_block",
    )(x, mod.arr, norm_post, a, w)


CONV_HIST = 32
LANES = 128


def _shifted_taps(ext_ref, w_ref, r0, cs, rc, hist_pad, off, taps, acc):
    n = rc + hist_pad
    win = ext_ref[pl.ds(r0, n), cs]
    for phase in range(8):
        wb = None
        for a8 in range(hist_pad // 8 + 1):
            j = 8 * a8 + phase - off
            if 0 <= j < taps:
                if wb is None:
                    wb = win if phase == 0 else pltpu.roll(win, n - phase, 0)
                acc = acc + w_ref[j:j + 1, cs] * wb[8 * a8:8 * a8 + rc]
    return acc


def _conv_a_kernel(val_ref, gate_ref, st_ref, w_ref, b_ref, lg_ref, lb_ref, a_ref, ns_ref,
                   ext_ref, y_ref, *, tt, rc, taps):
    t = pl.program_id(1)
    hist = taps - 1
    off = CONV_HIST - hist
    ch = val_ref.shape[-1]

    @pl.when(t == 0)
    def _():
        ext_ref[off:CONV_HIST, :] = st_ref[...]

    @pl.when(t > 0)
    def _():
        ext_ref[0:CONV_HIST, :] = ext_ref[tt:tt + CONV_HIST, :]

    ext_ref[CONV_HIST:CONV_HIST + tt, :] = val_ref[...] * jax.nn.sigmoid(gate_ref[...])
    if tt % rc:
        ext_ref[CONV_HIST + tt:, :] = jnp.zeros((ext_ref.shape[0] - CONV_HIST - tt, ch), F32)

    def chunk(r, carry):
        r0 = pl.multiple_of(r * rc, rc)
        for c in range(ch // LANES):
            cs = slice(c * LANES, (c + 1) * LANES)
            acc = _shifted_taps(ext_ref, w_ref, r0, cs, rc, CONV_HIST, off, taps,
                                jnp.broadcast_to(b_ref[:, cs], (rc, LANES)))
            rows = min(rc, tt)
            y_ref[pl.ds(r0, rows), cs] = acc[:rows]
        return carry

    lax.fori_loop(0, pl.cdiv(tt, rc), chunk, 0)

    y = y_ref[...]
    mu = jnp.mean(y, axis=-1, keepdims=True)
    d = y - mu
    var = jnp.mean(d * d, axis=-1, keepdims=True)
    a_ref[...] = _silu(d * lax.rsqrt(var + EPS) * lg_ref[...] + lb_ref[...]).astype(a_ref.dtype)

    @pl.when(t == pl.num_programs(1) - 1)
    def _():
        ns_ref[...] = ext_ref[tt + off:tt + CONV_HIST, :]


def conv_a_block(proj3, conv_state, conv_w, conv_b, ln_g, ln_b, tt, rc):
    B, T, _ = proj3.shape
    taps, CH = conv_w.shape
    row = lambda v: v.reshape(1, CH)
    return pl.pallas_call(
        functools.partial(_conv_a_kernel, tt=tt, rc=rc, taps=taps),
        grid=(B, T // tt),
        in_specs=[
            pl.BlockSpec((None, tt, CH), lambda b, t: (b, t, 0)),
            pl.BlockSpec((None, tt, CH), lambda b, t: (b, t, 1)),
            pl.BlockSpec((None, taps - 1, CH), lambda b, t: (b, 0, 0)),
            pl.BlockSpec((taps, CH), lambda b, t: (0, 0)),
            pl.BlockSpec((1, CH), lambda b, t: (0, 0)),
            pl.BlockSpec((1, CH), lambda b, t: (0, 0)),
            pl.BlockSpec((1, CH), lambda b, t: (0, 0)),
        ],
        out_specs=[
            pl.BlockSpec((None, tt, CH), lambda b, t: (b, t, 0)),
            pl.BlockSpec((None, taps - 1, CH), lambda b, t: (b, 0, 0)),
        ],
        out_shape=[jax.ShapeDtypeStruct((B, T, CH), BF16),
                   jax.ShapeDtypeStruct((B, taps - 1, CH), F32)],
        scratch_shapes=[pltpu.VMEM((CONV_HIST + pl.cdiv(tt, rc) * rc, CH), F32), pltpu.VMEM((tt, CH), F32)],
        compiler_params=_cparams(("parallel", "arbitrary")),
        name="conv_a_block",
    )(proj3, proj3, conv_state, conv_w, row(conv_b), row(ln_g), row(ln_b))


def _log_sigmoid_pair(z):
    l = jnp.log(1.0 + jnp.exp(-jnp.abs(z)))
    return jnp.minimum(z, 0.0) - l, jnp.minimum(-z, 0.0) - l


def _suffix_sum(lk, u2_ref):
    hi = lk.astype(BF16)
    lo = (lk - hi.astype(F32)).astype(BF16)
    return jnp.dot(jnp.concatenate([hi, lo], axis=1), u2_ref[...], preferred_element_type=F32)


def _sb_prompt_kernel(bias_ref, q_ref, k_ref, v_ref, u_ref, o_ref, k16_ref, v16_ref, *, tq, tk, scale):
    h = pl.program_id(1)
    i = pl.program_id(2)

    @pl.when(i == 0)
    def _():
        k16_ref[...] = k_ref[...].astype(BF16)
        v16_ref[...] = v_ref[...].astype(BF16)

    bias = bias_ref[h]
    q16 = q_ref[...].astype(BF16)
    nd = tq // tk

    def sweep(j_hi, m, acc, diagonal):
        starts = [j_hi - (n + 1) * tk for n in range(nd)]
        lss, lks = [], []
        for n, j0 in enumerate(starts):
            kb = k16_ref[pl.ds(j0, tk), :]
            z = lax.dot_general(q16, kb, (((1,), (1,)), ((), ())), preferred_element_type=F32) * scale + bias
            ls, lk = _log_sigmoid_pair(z)
            if diagonal:
                qpos = lax.broadcasted_iota(jnp.int32, (tq, tk), 0)
                kpos = lax.broadcasted_iota(jnp.int32, (tq, tk), 1) + (nd - 1 - n) * tk
                vis = kpos < qpos
                ls = jnp.where(vis, ls, -jnp.inf)
                lk = jnp.where(vis, lk, 0.0)
            lss.append(ls)
            lks.append(lk)
        laters = [_suffix_sum(lk, u_ref) for lk in lks]
        for n, j0 in enumerate(starts):
            w = jnp.exp(lss[n] + laters[n] + m)
            acc = acc + jnp.dot(w.astype(BF16), v16_ref[pl.ds(j0, tk), :], preferred_element_type=F32)
            m = m + laters[n][:, 0:1] + lks[n][:, 0:1]
        return m, acc

    base = pl.multiple_of(i * tq, tq)
    m, acc = sweep(base + tq, jnp.zeros((tq, 1), F32), jnp.zeros((tq, o_ref.shape[-1]), F32), True)

    def body(jj, carry):
        return sweep(pl.multiple_of(base - jj * tq, tq), carry[0], carry[1], False)

    m, acc = lax.fori_loop(0, i, body, (m, acc))
    o_ref[...] = acc.astype(o_ref.dtype)


def _suffix_matrix(n):
    r = lax.broadcasted_iota(jnp.int32, (2 * n, n), 0) % n
    c = lax.broadcasted_iota(jnp.int32, (2 * n, n), 1)
    return (r > c).astype(BF16)


def sb_attention_prompt(proj3, sb_bias, tq, tk, heads, q_col, k_col, v_col):
    B, T, _ = proj3.shape
    Dh = LANES
    return pl.pallas_call(
        functools.partial(_sb_prompt_kernel, tq=tq, tk=tk, scale=Dh ** -0.5),
        grid=(B, heads, T // tq),
        in_specs=[
            pl.BlockSpec(memory_space=pltpu.SMEM),
            pl.BlockSpec((None, tq, Dh), lambda b, h, i: (b, i, q_col + h)),
            pl.BlockSpec((None, T, Dh), lambda b, h, i: (b, 0, k_col + h)),
            pl.BlockSpec((None, T, Dh), lambda b, h, i: (b, 0, v_col + h)),
            pl.BlockSpec((2 * tk, tk), lambda b, h, i: (0, 0)),
        ],
        out_specs=pl.BlockSpec((None, tq, Dh), lambda b, h, i: (b, i, h)),
        out_shape=jax.ShapeDtypeStruct((B, T, heads * Dh), BF16),
        scratch_shapes=[pltpu.VMEM((T, Dh), BF16), pltpu.VMEM((T, Dh), BF16)],
        compiler_params=_cparams(("parallel", "parallel", "arbitrary")),
        name="sb_attention_prompt",
    )(sb_bias, proj3, proj3, proj3, _suffix_matrix(tk))


def _sb_sample_kernel(tbl_ref, q_ref, knew_ref, vnew_ref, bias_ref, u_ref, *rest, pp, page, heads, scale):
    page_refs = rest[:pp]
    o_ref, m_ref = rest[pp], rest[pp + 1]
    g = pl.program_id(1)
    R = q_ref.shape[0]
    q16 = q_ref[...].astype(BF16)
    bias = bias_ref[...]
    rhead = lax.broadcasted_iota(jnp.int32, (R, LANES), 0) % heads
    rtime = lax.broadcasted_iota(jnp.int32, (R, LANES), 0) // heads
    ccol = lax.broadcasted_iota(jnp.int32, (R, LANES), 1)
    same_head = (ccol % heads) == rhead

    def sweep(blocks, m, acc, vis):
        nch = blocks[0][0].shape[0] // LANES
        order = [(b, c) for b in range(len(blocks)) for c in reversed(range(nch))]
        lss, lks = {}, {}
        for b, (k2d, _) in enumerate(blocks):
            z = lax.dot_general(q16, k2d.astype(BF16), (((1,), (1,)), ((), ())),
                                preferred_element_type=F32) * scale + bias
            ls, lk = _log_sigmoid_pair(z)
            for c in range(nch):
                lss[b, c] = jnp.where(vis, ls[:, c * LANES:(c + 1) * LANES], -jnp.inf)
                lks[b, c] = jnp.where(vis, lk[:, c * LANES:(c + 1) * LANES], 0.0)
        later = _suffix_sum(jnp.concatenate([lks[bc] for bc in order], axis=0), u_ref)
        ws = {}
        for n_, bc in enumerate(order):
            lat = later[n_ * R:(n_ + 1) * R]
            ws[bc] = jnp.exp(lss[bc] + lat + m).astype(BF16)
            m = m + lat[:, 0:1] + lks[bc][:, 0:1]
        for b, (_, v2d) in enumerate(blocks):
            w = jnp.concatenate([ws[b, c] for c in range(nch)], axis=1)
            acc = acc + jnp.dot(w, v2d.astype(BF16), preferred_element_type=F32)
        return m, acc

    @pl.when(g == 0)
    def _():
        vis_new = same_head & ((ccol // heads) < rtime)
        m, acc = sweep([(knew_ref[...], vnew_ref[...])], jnp.zeros((R, 1), F32),
                       jnp.zeros(o_ref.shape, F32), vis_new)
        m_ref[...] = m
        o_ref[...] = acc

    blocks = [(pr[:, 0].reshape(page * heads, LANES), pr[:, 1].reshape(page * heads, LANES))
              for pr in page_refs]
    m, acc = sweep(blocks, m_ref[...], o_ref[...], same_head)
    m_ref[...] = m
    o_ref[...] = acc


def sb_attention_sample(q, k_new, v_new, cache_kv, layer_idx, page_table, sb_bias, pp):
    B, R, Dh = q.shape
    _, _, page, _, heads, _ = cache_kv.shape
    n_pages = page_table.shape[1]
    bias_col = jnp.tile(sb_bias, R // heads).reshape(R, 1)

    def page_spec(u):
        def imap(b, g, tbl):
            return (layer_idx, tbl[b, n_pages - 1 - (g * pp + u)], 0, 0, 0, 0)
        return pl.BlockSpec((None, None, page, 2, heads, Dh), imap)

    grid_spec = pltpu.PrefetchScalarGridSpec(
        num_scalar_prefetch=1,
        grid=(B, n_pages // pp),
        in_specs=[
            pl.BlockSpec((None, R, Dh), lambda b, g, tbl: (b, 0, 0)),
            pl.BlockSpec((None, LANES, Dh), lambda b, g, tbl: (b, 0, 0)),
            pl.BlockSpec((None, LANES, Dh), lambda b, g, tbl: (b, 0, 0)),
            pl.BlockSpec((R, 1), lambda b, g, tbl: (0, 0)),
            pl.BlockSpec((2 * LANES, LANES), lambda b, g, tbl: (0, 0)),
        ] + [page_spec(u) for u in range(pp)],
        out_specs=pl.BlockSpec((None, R, Dh), lambda b, g, tbl: (b, 0, 0)),
        scratch_shapes=[pltpu.VMEM((R, 1), F32)],
    )
    return pl.pallas_call(
        functools.partial(_sb_sample_kernel, pp=pp, page=page, heads=heads, scale=Dh ** -0.5),
        grid_spec=grid_spec,
        out_shape=jax.ShapeDtypeStruct((B, R, Dh), F32),
        compiler_params=_cparams(("parallel", "arbitrary")),
        name="sb_attention_sample",
    )(page_table, q, k_new, v_new, bias_col, _suffix_matrix(LANES), *([cache_kv] * pp))


GDN_HIST = 8


def _gdn_prep_kernel(x_ref, st_ref, w_ref, o_ref, ns_ref, ext_ref, *, tt, rc, taps, n_q, n_k, scale):
    j = pl.program_id(1)
    t = pl.program_id(2)
    hist = taps - 1
    off = GDN_HIST - hist
    tc = x_ref.shape[-1]

    @pl.when(t == 0)
    def _():
        ext_ref[off:GDN_HIST, :] = st_ref[...]

    @pl.when(t > 0)
    def _():
        ext_ref[0:GDN_HIST, :] = ext_ref[tt:tt + GDN_HIST, :]

    ext_ref[GDN_HIST:GDN_HIST + tt, :] = x_ref[...]
    if tt % rc:
        ext_ref[GDN_HIST + tt:, :] = jnp.zeros((ext_ref.shape[0] - GDN_HIST - tt, tc), F32)

    rows = min(rc, tt)

    def run(normalise, post_scale):
        def chunk(r, carry):
            r0 = pl.multiple_of(r * rc, rc)
            for c in range(tc // LANES):
                cs = slice(c * LANES, (c + 1) * LANES)
                y = _silu(_shifted_taps(ext_ref, w_ref, r0, cs, rc, GDN_HIST, off, taps,
                                        jnp.zeros((rc, LANES), F32)))
                if normalise:
                    y = y * lax.rsqrt(jnp.sum(y * y, axis=-1, keepdims=True) + EPS)
                    if post_scale != 1.0:
                        y = y * post_scale
                o_ref[pl.ds(r0, rows), cs] = y[:rows]
            return carry
        lax.fori_loop(0, pl.cdiv(tt, rc), chunk, 0)

    @pl.when(j < n_q)
    def _():
        run(True, scale)

    @pl.when((j >= n_q) & (j < n_q + n_k))
    def _():
        run(True, 1.0)

    @pl.when(j >= n_q + n_k)
    def _():
        run(False, 1.0)

    @pl.when(t == pl.num_programs(2) - 1)
    def _():
        ns_ref[...] = ext_ref[tt + off:tt + GDN_HIST, :]


def gdn_prep_block(proj3, conv_state, conv_w, kw, tt, rc, tc):
    B, T, _ = proj3.shape
    taps, QKV = conv_w.shape
    return pl.pallas_call(
        functools.partial(_gdn_prep_kernel, tt=tt, rc=rc, taps=taps, n_q=kw // tc, n_k=kw // tc,
                          scale=LANES ** -0.5),
        grid=(B, QKV // tc, T // tt),
        in_specs=[
            pl.BlockSpec((None, tt, tc), lambda b, j, t: (b, t, j)),
            pl.BlockSpec((None, taps - 1, tc), lambda b, j, t: (b, 0, j)),
            pl.BlockSpec((taps, tc), lambda b, j, t: (0, j)),
        ],
        out_specs=[
            pl.BlockSpec((None, tt, tc), lambda b, j, t: (b, t, j)),
            pl.BlockSpec((None, taps - 1, tc), lambda b, j, t: (b, 0, j)),
        ],
        out_shape=[jax.ShapeDtypeStruct((B, T, QKV), F32),
                   jax.ShapeDtypeStruct((B, taps - 1, QKV), F32)],
        scratch_shapes=[pltpu.VMEM((GDN_HIST + pl.cdiv(tt, rc) * rc, tc), F32)],
        compiler_params=_cparams(("parallel", "parallel", "arbitrary")),
        name="gdn_prep_block",
    )(proj3, conv_state, conv_w)


def _split3(x):
    hi = x.astype(BF16)
    r = x - hi.astype(F32)
    mid = r.astype(BF16)
    lo = (r - mid.astype(F32)).astype(BF16)
    return hi, mid, lo


def _gdn_gates_kernel(ba_ref, alog_ref, dtb_ref, tri_ref, o_ref, *, nh, seq_rows, valid_rows):
    ba = ba_ref[...]
    R = ba.shape[0]
    beta = jax.nn.sigmoid(ba[:, :nh])
    x = ba[:, nh:] + dtb_ref[...]
    softplus = jnp.maximum(x, 0.0) + jnp.log1p(jnp.exp(-jnp.abs(x)))
    g = -jnp.exp(alog_ref[...]) * softplus
    if valid_rows < seq_rows:
        row = lax.broadcasted_iota(jnp.int32, (R, nh), 0) % seq_rows
        beta = jnp.where(row < valid_rows, beta, 0.0)
        g = jnp.where(row < valid_rows, g, 0.0)
    tri = tri_ref[...]
    gc = sum(jnp.dot(tri, p, preferred_element_type=F32) for p in _split3(g))
    o_ref[...] = jnp.concatenate([gc, beta, jnp.zeros((R, LANES - 2 * nh), F32)], axis=1)


def gdn_gates(ba, a_log, dt_bias, chunk, seq_rows, valid_rows, tr):
    M, two_nh = ba.shape
    nh = two_nh // 2
    r = lax.broadcasted_iota(jnp.int32, (tr, tr), 0)
    c = lax.broadcasted_iota(jnp.int32, (tr, tr), 1)
    tri = ((r // chunk == c // chunk) & (r >= c)).astype(BF16)
    return pl.pallas_call(
        functools.partial(_gdn_gates_kernel, nh=nh, seq_rows=seq_rows, valid_rows=valid_rows),
        grid=(M // tr,),
        in_specs=[
            pl.BlockSpec((tr, two_nh), lambda i: (i, 0)),
            pl.BlockSpec((1, nh), lambda i: (0, 0)),
            pl.BlockSpec((1, nh), lambda i: (0, 0)),
            pl.BlockSpec((tr, tr), lambda i: (0, 0)),
        ],
        out_specs=pl.BlockSpec((tr, LANES), lambda i: (i, 0)),
        out_shape=jax.ShapeDtypeStruct((M, LANES), F32),
        compiler_params=_cparams(("parallel",)),
        name="gdn_gates",
    )(ba, a_log.reshape(1, nh), dt_bias.reshape(1, nh), tri)


def _mm(a, b):
    return jnp.dot(a.astype(BF16), b.astype(BF16), preferred_element_type=F32)


def _gdn_chunk_kernel(q_ref, k_ref, v_ref, z_ref, gc_ref, bt_ref, s0_ref, ng_ref, o_ref, so_ref, s_ref,
                      *, nc, hk, C):
    t = pl.program_id(2)
    Dh = LANES
    C2 = 2 * C

    @pl.when(t == 0)
    def _():
        s_ref[...] = s0_ref[...]

    ii = lax.broadcasted_iota(jnp.int32, (C, C2), 0)
    ll = lax.broadcasted_iota(jnp.int32, (C, C2), 1)
    jj = ll % C
    left = ll < C
    causal = ii >= jj
    strict = ii > jj
    diag = ii == jj
    eye2 = diag.astype(F32)
    nt = (((1,), (1,)), ((), ()))
    tn = (((0,), (0,)), ((), ()))

    def block_diag(p2):
        return jnp.concatenate([jnp.where(left, p2, 0.0), jnp.where(left, 0.0, p2)], axis=0).astype(BF16)

    units = [(kk, c) for kk in range(hk) for c in range(nc)]
    kq, gcol, bcol, grow2, decay2, attn2, pw, xinv = {}, {}, {}, {}, {}, {}, {}, {}

    for un in units:
        kk, c = un
        rs = slice(c * C, (c + 1) * C)
        ks = slice(kk * Dh, (kk + 1) * Dh)
        kc, qc = k_ref[rs, ks], q_ref[rs, ks]
        kq[un] = (kc, qc)
        k16 = kc.astype(BF16)
        gkk = lax.dot_general(k16, k16, nt, preferred_element_type=F32)
        gqk = lax.dot_general(qc.astype(BF16), k16, nt, preferred_element_type=F32)
        g2 = jnp.concatenate([gc_ref[2 * kk, :, rs], gc_ref[2 * kk + 1, :, rs]], axis=1)
        b2 = jnp.concatenate([bt_ref[2 * kk, :, rs], bt_ref[2 * kk + 1, :, rs]], axis=1)
        gd, bd = jnp.where(diag, g2, 0.0), jnp.where(diag, b2, 0.0)
        gcol[un] = [jnp.sum(gd[:, e * C:(e + 1) * C], axis=1, keepdims=True) for e in range(2)]
        bcol[un] = [jnp.sum(bd[:, e * C:(e + 1) * C], axis=1, keepdims=True) for e in range(2)]
        gcol2 = jnp.where(left, gcol[un][0], gcol[un][1])
        bcol2 = jnp.where(left, bcol[un][0], bcol[un][1])
        grow2[un] = g2
        decay2[un] = jnp.exp(jnp.where(causal, gcol2 - g2, -jnp.inf))
        a2 = jnp.where(strict, jnp.concatenate([gkk, gkk], axis=1) * bcol2 * decay2[un], 0.0)
        attn2[un] = jnp.concatenate([gqk, gqk], axis=1) * decay2[un]
        pw[un] = a2
        xinv[un] = eye2 - a2

    k = 2
    while k < C:
        for un in units:
            pw[un] = jnp.dot(pw[un].astype(BF16), block_diag(pw[un]), preferred_element_type=F32)
        for un in units:
            xinv[un] = xinv[un] + jnp.dot(xinv[un].astype(BF16), block_diag(pw[un]),
                                          preferred_element_type=F32)
        k *= 2

    heads = [(kk, e) for kk in range(hk) for e in range(2)]
    uw, qd, kd, gl = {}, {}, {}, {}
    for un in units:
        kk, c = un
        rs = slice(c * C, (c + 1) * C)
        kc, qc = kq[un]
        for e in range(2):
            vs = slice((2 * kk + e) * Dh, (2 * kk + e + 1) * Dh)
            bc, gcl = bcol[un][e], gcol[un][e]
            egc = jnp.exp(gcl)
            glast = grow2[un][:, e * C + C - 1:e * C + C]
            rhs = jnp.concatenate([v_ref[rs, vs] * bc, kc * (bc * egc)], axis=1)
            uw[kk, e, c] = _mm(xinv[un][:, e * C:(e + 1) * C], rhs)
            qd[kk, e, c] = (qc * egc).astype(BF16)
            kd[kk, e, c] = (kc * jnp.exp(glast - gcl)).astype(BF16)
            gl[kk, e, c] = jnp.exp(glast)

    S = {h: s_ref[2 * h[0] + h[1]] for h in heads}
    for c in range(nc):
        rs = slice(c * C, (c + 1) * C)
        s16 = {h: S[h].astype(BF16) for h in heads}
        v_new = {h: uw[h[0], h[1], c][:, :Dh]
                 - jnp.dot(uw[h[0], h[1], c][:, Dh:].astype(BF16), s16[h], preferred_element_type=F32)
                 for h in heads}
        for h in heads:
            kk, e = h
            S[h] = S[h] * gl[kk, e, c] + lax.dot_general(kd[kk, e, c], v_new[h].astype(BF16), tn,
                                                          preferred_element_type=F32)
        for h in heads:
            kk, e = h
            vs = slice((2 * kk + e) * Dh, (2 * kk + e + 1) * Dh)
            o = (jnp.dot(qd[kk, e, c], s16[h], preferred_element_type=F32)
                 + _mm(attn2[kk, c][:, e * C:(e + 1) * C], v_new[h]))
            o = _rms(o) * ng_ref[...] * _silu(z_ref[rs, vs])
            o_ref[rs, vs] = o.astype(o_ref.dtype)
    for h in heads:
        s_ref[2 * h[0] + h[1]] = S[h]

    @pl.when(t == pl.num_programs(2) - 1)
    def _():
        so_ref[...] = s_ref[...]


def gdn_chunk_block(qkv, proj3, gates_t, s0, norm_g, kw, z_col0, chunk, nc, hk):
    B, T, QKV = qkv.shape
    Dh = LANES
    nh = s0.shape[1]
    kh = kw // Dh
    tcs = nc * chunk
    qw, vw = hk * Dh, 2 * hk * Dh
    return pl.pallas_call(
        functools.partial(_gdn_chunk_kernel, nc=nc, hk=hk, C=chunk),
        grid=(B, kh // hk, T // tcs),
        in_specs=[
            pl.BlockSpec((None, tcs, qw), lambda b, g, t: (b, t, g)),
            pl.BlockSpec((None, tcs, qw), lambda b, g, t: (b, t, kw // qw + g)),
            pl.BlockSpec((None, tcs, vw), lambda b, g, t: (b, t, 2 * kw // vw + g)),
            pl.BlockSpec((None, tcs, vw), lambda b, g, t: (b, t, z_col0 // vw + g)),
            pl.BlockSpec((None, 2 * hk, 1, tcs), lambda b, g, t: (b, g, 0, t)),
            pl.BlockSpec((None, 2 * hk, 1, tcs), lambda b, g, t: (b, nh // (2 * hk) + g, 0, t)),
            pl.BlockSpec((None, 2 * hk, Dh, Dh), lambda b, g, t: (b, g, 0, 0)),
            pl.BlockSpec((1, Dh), lambda b, g, t: (0, 0)),
        ],
        out_specs=[
            pl.BlockSpec((None, tcs, vw), lambda b, g, t: (b, t, g)),
            pl.BlockSpec((None, 2 * hk, Dh, Dh), lambda b, g, t: (b, g, 0, 0)),
        ],
        out_shape=[jax.ShapeDtypeStruct((B, T, nh * Dh), BF16),
                   jax.ShapeDtypeStruct((B, nh, Dh, Dh), F32)],
        scratch_shapes=[pltpu.VMEM((2 * hk, Dh, Dh), F32)],
        compiler_params=_cparams(("parallel", "parallel", "arbitrary")),
        name="gdn_chunk_block",
    )(qkv, qkv, qkv, proj3, gates_t, gates_t, s0, norm_g.reshape(1, Dh))


GDN_CHUNK = 64


class _Cfg:
    def __init__(self, prompt):
        self.prompt = prompt
        if prompt:
            self.tm, self.conv_tt, self.conv_rc = 512, 256, 64
            self.gdn_tt, self.gdn_rc, self.gdn_nc, self.gdn_hk = 256, 64, 4, 2
        else:
            self.tm, self.conv_tt, self.conv_rc = None, None, 8
            self.gdn_tt, self.gdn_rc, self.gdn_nc, self.gdn_hk = None, 8, 1, 4


def _conv_sb_mixer(proj3, i, cfg, conv_state, cache_kv, table, P):
    B, T, _ = proj3.shape
    ch = P['conv_a_w'].shape[-1]
    heads, Dh = cache_kv.shape[-2], cache_kv.shape[-1]
    sbw = heads * Dh
    qb, kb, vb = 2 * ch // Dh, (2 * ch + sbw) // Dh, (2 * ch + 2 * sbw) // Dh
    a16, cs = conv_a_block(proj3, conv_state, P['conv_a_w'][i], P['conv_a_b'][i], P['ln_a_g'][i],
                           P['ln_a_b'][i], cfg.conv_tt or T, cfg.conv_rc)
    kv = proj3[:, :, 2 * ch + sbw:].reshape(B, T, 2, heads, Dh)
    if cfg.prompt:
        o16 = sb_attention_prompt(proj3, P['sb_bias'][i], 512, 256, heads, qb, kb, vb)
    else:
        rows = T * heads
        q = proj3[:, :, 2 * ch:2 * ch + sbw].reshape(B, rows, Dh)
        pad = lambda x: jnp.pad(x.reshape(B, rows, Dh), ((0, 0), (0, LANES - rows), (0, 0)))
        o = sb_attention_sample(q, pad(kv[:, :, 0]), pad(kv[:, :, 1]), cache_kv, i, table, P['sb_bias'][i], 8)
        o16 = o.reshape(B, T, sbw).astype(BF16)
    cat = jnp.concatenate([a16, o16], axis=-1)
    return cat.reshape(B * T, ch + sbw), cs, kv


def _gdn_mixer(proj3, ba, i, cfg, conv_state, s0, P):
    B, T, _ = proj3.shape
    nh = s0.shape[1]
    kw = P['w_out_gdn'].shape[1] // 2
    qkv_w = P['conv_c_w'].shape[-1]
    qkv, cs = gdn_prep_block(proj3, conv_state, P['conv_c_w'][i], kw, cfg.gdn_tt or T, cfg.gdn_rc, 1024)
    if T % GDN_CHUNK:
        tp = -(-T // GDN_CHUNK) * GDN_CHUNK
        padt = lambda x: jnp.pad(x, ((0, 0), (0, tp - T), (0, 0)))
        qkv, zsrc, z_col0, ba = padt(qkv), padt(proj3[:, :, qkv_w:]), 0, padt(ba)
    else:
        tp, zsrc, z_col0 = T, proj3, qkv_w
    gates = gdn_gates(ba.reshape(B * tp, 2 * nh), P['a_log'][i], P['dt_bias'][i], GDN_CHUNK, tp, T,
                      min(512, B * tp))
    gates_t = gates.reshape(B, tp, LANES)[:, :, :2 * nh].transpose(0, 2, 1).reshape(B, 2 * nh, 1, tp)
    og, s = gdn_chunk_block(qkv, zsrc, gates_t, s0, P['gdn_norm_g'][i], kw, z_col0, GDN_CHUNK, cfg.gdn_nc,
                            cfg.gdn_hk)
    return og[:, :T].reshape(B * T, -1), cs, s


def _trunk(x, mod, B, T, cfg, cache_kv, table, conv_a_state, conv_c_state, gdn_state, P):
    D = x.shape[-1]
    depth = P['norm_pre'].shape[0]
    tm = cfg.tm or B * T
    tf = 512
    kv_rows, conv_a_new, conv_c_new, gdn_new = [], [], [], []
    for l in range(depth):
        i = l // 2
        x = ffn_block(x, mod, P['norm_pre'], P['norm_post'], P['wg'], P['wu'], P['wd'], l, 0, 0, tm, tf)
        if l % 2 == 0:
            proj = inproj_block(x, mod, P['norm_pre'], P['w_in_ab'], l, 1, tm, 1024)
            cat, cs, kv = _conv_sb_mixer(proj.reshape(B, T, -1), i, cfg, conv_a_state[i], cache_kv, table, P)
            conv_a_new.append(cs)
            kv_rows.append(kv)
            x = outproj_block(x, cat, mod, P['norm_post'], P['w_out_ab'], l, 1, tm, 1024)
        else:
            proj, ba = inproj_block(x, mod, P['norm_pre'], P['w_in_gdn'], l, 1, tm, 1024, w_extra=P['w_in_gdn_ba'])
            og, cs, s = _gdn_mixer(proj.reshape(B, T, -1), ba.reshape(B, T, -1), i, cfg, conv_c_state[i],
                                   gdn_state[i], P)
            conv_c_new.append(cs)
            gdn_new.append(s)
            x = outproj_block(x, og, mod, P['norm_post'], P['w_out_gdn'], l, 1, tm, 1024)
        x = ffn_block(x, mod, P['norm_pre'], P['norm_post'], P['wg'], P['wu'], P['wd'], l, 2, 1, tm, tf)
    return (x.reshape(B, T, D), jnp.stack(kv_rows), jnp.stack(conv_a_new), jnp.stack(conv_c_new),
            jnp.stack(gdn_new))


def kernel(x_prompt, x_sample, cache_kv_sb, state_conv_a, state_conv_c, state_gdn, page_table,
           c_prompt, c_sample, w_ada, b_ada, norm_pre, norm_post, w_ffn_gate, w_ffn_up, w_ffn_down,
           w_in_ab, conv_a_w, conv_a_b, ln_a_g, ln_a_b, w_out_ab, sb_bias,
           w_in_gdn, conv_c_w, a_log, dt_bias, gdn_norm_g, w_out_gdn):
    BP, T, D = x_prompt.shape
    BS, TS, _ = x_sample.shape
    L = w_ada.shape[0]
    n_sb, n_gdn = w_in_ab.shape[0], w_in_gdn.shape[0]
    dt = x_prompt.dtype

    R = -(-(BP + BS) // 8) * 8
    c_all = jnp.concatenate([c_prompt, c_sample, jnp.zeros((R - BP - BS, D), dt)], axis=0)
    mod_all = ada_modulation(c_all, w_ada, b_ada)
    mod_p = _Mod(mod_all[:, :, :BP].reshape(L, 9, BP, 1, D), T, False)
    mod_s = _Mod(jnp.repeat(mod_all[:, :, BP:BP + BS], TS, axis=2), TS, True)

    qkvz = conv_c_w.shape[-1] + w_out_gdn.shape[1]
    P = dict(
        norm_pre=norm_pre.reshape(L, 3, 1, D), norm_post=norm_post.reshape(L, 3, 1, D),
        wg=w_ffn_gate.astype(BF16), wu=w_ffn_up.astype(BF16), wd=w_ffn_down.astype(BF16),
        w_in_ab=w_in_ab.astype(BF16), w_out_ab=w_out_ab.astype(BF16),
        w_in_gdn=w_in_gdn.astype(BF16), w_in_gdn_ba=w_in_gdn[:, :, qkvz:].astype(BF16),
        w_out_gdn=w_out_gdn.astype(BF16),
        conv_a_w=conv_a_w, conv_a_b=conv_a_b, ln_a_g=ln_a_g, ln_a_b=ln_a_b, sb_bias=sb_bias,
        conv_c_w=conv_c_w, a_log=a_log, dt_bias=dt_bias, gdn_norm_g=gdn_norm_g,
    )

    y_p, kv_p, ca_p, cc_p, g_p = _trunk(
        x_prompt.reshape(BP * T, D), mod_p, BP, T, _Cfg(True), cache_kv_sb, None,
        jnp.zeros((n_sb, BP) + state_conv_a.shape[2:], dt), jnp.zeros((n_gdn, BP) + state_conv_c.shape[2:], dt),
        jnp.zeros((n_gdn, BP) + state_gdn.shape[2:], dt), P)
    y_s, kv_s, ca_s, cc_s, g_s = _trunk(
        x_sample.reshape(BS * TS, D), mod_s, BS, TS, _Cfg(False), cache_kv_sb, page_table,
        state_conv_a, state_conv_c, state_gdn, P)
    return (y_p, y_s, kv_p, ca_p, cc_p, g_p, kv_s, ca_s, cc_s, g_s)
```

```python
import functools
import math

import jax
import jax.numpy as jnp
from jax import lax
from jax.experimental import pallas as pl
from jax.experimental.pallas import tpu as pltpu

F32 = jnp.float32
BF16 = jnp.bfloat16
EPS = 1e-6
FFN_RESIDUAL = 0.5
VMEM_LIMIT = 56 * 1024 * 1024


def _cparams(sem):
    return pltpu.CompilerParams(dimension_semantics=sem, vmem_limit_bytes=VMEM_LIMIT)


def _silu(x):
    return x * jax.nn.sigmoid(x)


def _rms(x):
    return x * lax.rsqrt(jnp.mean(x * x, axis=-1, keepdims=True) + EPS)


ROWS = 16
ROWS_UNROLL = 4


def _rows_loop(n_rows, body):
    def step(r, carry):
        body(pl.multiple_of(r * ROWS, ROWS))
        return carry
    n = n_rows // ROWS
    lax.fori_loop(0, n, step, 0, unroll=math.gcd(n, ROWS_UNROLL))


def _mod_rows(ref, r0):
    return ref[...] if ref.shape[0] == 1 else ref[pl.ds(r0, ROWS), :]


def _prenorm_rows(x_ref, sh_ref, sc_ref, gpre_ref, h_ref):
    def body(r0):
        y = _rms(x_ref[pl.ds(r0, ROWS), :]) * gpre_ref[...]
        h_ref[pl.ds(r0, ROWS), :] = (y * (1.0 + _mod_rows(sc_ref, r0)) + _mod_rows(sh_ref, r0)).astype(BF16)
    _rows_loop(x_ref.shape[0], body)


def _postnorm_rows(acc_ref, x_ref, gt_ref, gpost_ref, o_ref, weight):
    def body(r0):
        rs = pl.ds(r0, ROWS)
        y = _rms(acc_ref[rs, :]) * gpost_ref[...]
        o_ref[rs, :] = x_ref[rs, :] + weight * (1.0 + _mod_rows(gt_ref, r0)) * y
    _rows_loop(x_ref.shape[0], body)


def _ada_kernel(c_ref, w_ref, b_ref, o_ref):
    s = _silu(c_ref[...]).astype(BF16)
    o_ref[...] = jnp.dot(s, w_ref[...].astype(BF16), preferred_element_type=F32) + b_ref[...]


def ada_modulation(c_all, w_ada, b_ada):
    L, D, N = w_ada.shape
    R = c_all.shape[0]
    nj = N // D
    return pl.pallas_call(
        _ada_kernel,
        grid=(L, nj),
        in_specs=[
            pl.BlockSpec((R, D), lambda l, j: (0, 0)),
            pl.BlockSpec((None, D, D), lambda l, j: (l, 0, j)),
            pl.BlockSpec((None, 1, D), lambda l, j: (l, 0, j)),
        ],
        out_specs=pl.BlockSpec((None, None, R, D), lambda l, j: (l, j, 0, 0)),
        out_shape=jax.ShapeDtypeStruct((L, nj, R, D), F32),
        compiler_params=_cparams(("parallel", "parallel")),
        name="ada_modulation",
    )(c_all, w_ada, b_ada.reshape(L, 1, N))


class _Mod:
    def __init__(self, arr, rows_per_seq, per_row):
        self.arr = arr
        self.rows_per_seq = rows_per_seq
        self.per_row = per_row

    def spec(self, l, k, tm, ngrid, tile=lambda i: i):
        D = self.arr.shape[-1]
        if self.per_row:
            def imap(i, *_):
                return (l, k, tile(i), 0)
            return pl.BlockSpec((None, None, tm, D), imap)
        rps = self.rows_per_seq

        def imap(i, *_):
            return (l, k, (tile(i) * tm) // rps, 0, 0)
        return pl.BlockSpec((None, None, None, 1, D), imap)


def _vec_spec(l, s, D):
    return pl.BlockSpec((None, None, 1, D), lambda i, *_: (l, s, 0, 0))


def _ffn_kernel(xn_ref, xp_ref, sh_ref, sc_ref, sh0_ref, sc0_ref, gt_ref, gpre_ref, gpost_ref,
                wg_ref, wu_ref, wd_ref, o_ref, h_ref, acc_ref, *, weight, nt, rp):
    i = pl.program_id(0)
    f = pl.program_id(1)
    tm = xn_ref.shape[0]

    @pl.when((i == 0) & (f == 0))
    def _():
        _prenorm_rows(xp_ref, sh0_ref, sc0_ref, gpre_ref, h_ref.at[0])
        acc_ref[1] = jnp.zeros((tm, acc_ref.shape[-1]), F32)

    @pl.when(i < nt)
    def _():
        slot = i % 2
        other = 1 - slot

        @pl.when(f == 0)
        def _():
            acc_ref[slot] = jnp.zeros((tm, acc_ref.shape[-1]), F32)

        h = h_ref[slot]
        g = jnp.dot(h, wg_ref[...], preferred_element_type=F32)
        u = jnp.dot(h, wu_ref[...], preferred_element_type=F32)
        a = (_silu(g) * u).astype(BF16)
        acc_ref[slot] += jnp.dot(a, wd_ref[...], preferred_element_type=F32)

        base = jnp.minimum(f * rp, tm - rp)
        for r in range(rp // ROWS):
            r0 = pl.multiple_of(base + r * ROWS, ROWS)
            rs = pl.ds(r0, ROWS)
            y = _rms(xn_ref[rs, :]) * gpre_ref[...]
            h_ref[other, rs, :] = (y * (1.0 + _mod_rows(sc_ref, r0)) + _mod_rows(sh_ref, r0)).astype(BF16)
            y = _rms(acc_ref[other, rs, :]) * gpost_ref[...]
            o_ref[rs, :] = xp_ref[rs, :] + weight * (1.0 + _mod_rows(gt_ref, r0)) * y

    @pl.when((i == nt) & (f == 0))
    def _():
        _postnorm_rows(acc_ref.at[(nt - 1) % 2], xp_ref, gt_ref, gpost_ref, o_ref, weight)


def ffn_block(x, mod, norm_pre, norm_post, wg, wu, wd, l, s, half, tm, tf):
    M, D = x.shape
    F = wg.shape[-1]
    nt, nf = M // tm, F // tf
    grid = (nt + 1, nf)
    k0 = 3 * s
    rp = -(-tm // (nf * ROWS)) * ROWS
    nxt = lambda i: jnp.minimum(i + 1, nt - 1)
    prv = lambda i: jnp.maximum(i - 1, 0)
    wcol = lambda i, f: jnp.where(i == nt, nf - 1, f)
    return pl.pallas_call(
        functools.partial(_ffn_kernel, weight=FFN_RESIDUAL, nt=nt, rp=rp),
        grid=grid,
        in_specs=[
            pl.BlockSpec((tm, D), lambda i, f: (nxt(i), 0)),
            pl.BlockSpec((tm, D), lambda i, f: (prv(i), 0)),
            mod.spec(l, k0 + 0, tm, grid, nxt), mod.spec(l, k0 + 1, tm, grid, nxt),
            mod.spec(l, k0 + 0, tm, grid, lambda i: 0), mod.spec(l, k0 + 1, tm, grid, lambda i: 0),
            mod.spec(l, k0 + 2, tm, grid, prv),
            _vec_spec(l, s, D), _vec_spec(l, s, D),
            pl.BlockSpec((None, None, D, tf), lambda i, f: (l, half, 0, wcol(i, f))),
            pl.BlockSpec((None, None, D, tf), lambda i, f: (l, half, 0, wcol(i, f))),
            pl.BlockSpec((None, None, tf, D), lambda i, f: (l, half, wcol(i, f), 0)),
        ],
        out_specs=pl.BlockSpec((tm, D), lambda i, f: (prv(i), 0)),
        out_shape=jax.ShapeDtypeStruct((M, D), F32),
        scratch_shapes=[pltpu.VMEM((2, tm, D), BF16), pltpu.VMEM((2, tm, D), F32)],
        compiler_params=_cparams(("arbitrary", "arbitrary")),
        name="ffn_block",
    )(x, x, mod.arr, mod.arr, mod.arr, mod.arr, mod.arr, norm_pre, norm_post, wg, wu, wd)


def _inproj_kernel(x_ref, sh_ref, sc_ref, gpre_ref, w_ref, *rest, has_extra):
    if has_extra:
        we_ref, o_ref, oe_ref, h_ref = rest
    else:
        o_ref, h_ref = rest
    j = pl.program_id(1)

    @pl.when(j == 0)
    def _():
        _prenorm_rows(x_ref, sh_ref, sc_ref, gpre_ref, h_ref)
        if has_extra:
            oe_ref[...] = jnp.dot(h_ref[...], we_ref[...], preferred_element_type=F32)

    o_ref[...] = jnp.dot(h_ref[...], w_ref[...], preferred_element_type=F32)


def inproj_block(x, mod, norm_pre, w, l, s, tm, tn, w_extra=None):
    M, D = x.shape
    N = w.shape[-1] // tn * tn
    i_kind = l // 2
    grid = (M // tm, N // tn)
    k0 = 3 * s
    in_specs = [
        pl.BlockSpec((tm, D), lambda i, j: (i, 0)),
        mod.spec(l, k0 + 0, tm, grid), mod.spec(l, k0 + 1, tm, grid),
        _vec_spec(l, s, D),
        pl.BlockSpec((None, D, tn), lambda i, j: (i_kind, 0, j)),
    ]
    args = [x, mod.arr, mod.arr, norm_pre, w]
    out_specs = [pl.BlockSpec((tm, tn), lambda i, j: (i, j))]
    out_shape = [jax.ShapeDtypeStruct((M, N), F32)]
    if w_extra is not None:
        ne = w_extra.shape[-1]
        in_specs.append(pl.BlockSpec((None, D, ne), lambda i, j: (i_kind, 0, 0)))
        args.append(w_extra)
        out_specs.append(pl.BlockSpec((tm, ne), lambda i, j: (i, 0)))
        out_shape.append(jax.ShapeDtypeStruct((M, ne), F32))
    res = pl.pallas_call(
        functools.partial(_inproj_kernel, has_extra=w_extra is not None),
        grid=grid,
        in_specs=in_specs,
        out_specs=out_specs,
        out_shape=out_shape,
        scratch_shapes=[pltpu.VMEM((tm, D), BF16)],
        compiler_params=_cparams(("parallel", "arbitrary")),
        name="inproj_block",
    )(*args)
    return res if w_extra is not None else res[0]


def _outproj_kernel(x_ref, gt_ref, gpost_ref, a_ref, w_ref, o_ref, acc_ref):
    k = pl.program_id(1)

    @pl.when(k == 0)
    def _():
        acc_ref[...] = jnp.zeros_like(acc_ref)

    acc_ref[...] += jnp.dot(a_ref[...], w_ref[...], preferred_element_type=F32)

    @pl.when(k == pl.num_programs(1) - 1)
    def _():
        _postnorm_rows(acc_ref, x_ref, gt_ref, gpost_ref, o_ref, 1.0)


def outproj_block(x, a, mod, norm_post, w, l, s, tm, tk):
    M, D = x.shape
    K = a.shape[-1]
    i_kind = l // 2
    grid = (M // tm, K // tk)
    return pl.pallas_call(
        _outproj_kernel,
        grid=grid,
        in_specs=[
            pl.BlockSpec((tm, D), lambda i, k: (i, 0)),
            mod.spec(l, 3 * s + 2, tm, grid),
            _vec_spec(l, s, D),
            pl.BlockSpec((tm, tk), lambda i, k: (i, k)),
            pl.BlockSpec((None, tk, D), lambda i, k: (i_kind, k, 0)),
        ],
        out_specs=pl.BlockSpec((tm, D), lambda i, k: (i, 0)),
        out_shape=jax.ShapeDtypeStruct((M, D), F32),
        scratch_shapes=[pltpu.VMEM((tm, D), F32)],
        compiler_params=_cparams(("parallel", "arbitrary")),
        name="outproj_block",
    )(x, mod.arr, norm_post, a, w)


CONV_HIST = 32
LANES = 128


def _shifted_taps(ext_ref, w_ref, r0, cs, rc, hist_pad, off, taps, acc):
    n = rc + hist_pad
    win = ext_ref[pl.ds(r0, n), cs]
    for phase in range(8):
        wb = None
        for a8 in range(hist_pad // 8 + 1):
            j = 8 * a8 + phase - off
            if 0 <= j < taps:
                if wb is None:
                    wb = win if phase == 0 else pltpu.roll(win, n - phase, 0)
                acc = acc + w_ref[j:j + 1, cs] * wb[8 * a8:8 * a8 + rc]
    return acc


def _conv_a_kernel(val_ref, gate_ref, st_ref, w_ref, b_ref, lg_ref, lb_ref, a_ref, ns_ref,
                   ext_ref, y_ref, *, tt, rc, taps):
    t = pl.program_id(1)
    hist = taps - 1
    off = CONV_HIST - hist
    ch = val_ref.shape[-1]

    @pl.when(t == 0)
    def _():
        ext_ref[off:CONV_HIST, :] = st_ref[...]

    @pl.when(t > 0)
    def _():
        ext_ref[0:CONV_HIST, :] = ext_ref[tt:tt + CONV_HIST, :]

    ext_ref[CONV_HIST:CONV_HIST + tt, :] = val_ref[...] * jax.nn.sigmoid(gate_ref[...])
    if tt % rc:
        ext_ref[CONV_HIST + tt:, :] = jnp.zeros((ext_ref.shape[0] - CONV_HIST - tt, ch), F32)

    def chunk(r, carry):
        r0 = pl.multiple_of(r * rc, rc)
        for c in range(ch // LANES):
            cs = slice(c * LANES, (c + 1) * LANES)
            acc = _shifted_taps(ext_ref, w_ref, r0, cs, rc, CONV_HIST, off, taps,
                                jnp.broadcast_to(b_ref[:, cs], (rc, LANES)))
            rows = min(rc, tt)
            y_ref[pl.ds(r0, rows), cs] = acc[:rows]
        return carry

    lax.fori_loop(0, pl.cdiv(tt, rc), chunk, 0)

    y = y_ref[...]
    mu = jnp.mean(y, axis=-1, keepdims=True)
    d = y - mu
    var = jnp.mean(d * d, axis=-1, keepdims=True)
    a_ref[...] = _silu(d * lax.rsqrt(var + EPS) * lg_ref[...] + lb_ref[...]).astype(a_ref.dtype)

    @pl.when(t == pl.num_programs(1) - 1)
    def _():
        ns_ref[...] = ext_ref[tt + off:tt + CONV_HIST, :]


def conv_a_block(proj3, conv_state, conv_w, conv_b, ln_g, ln_b, tt, rc):
    B, T, _ = proj3.shape
    taps, CH = conv_w.shape
    row = lambda v: v.reshape(1, CH)
    return pl.pallas_call(
        functools.partial(_conv_a_kernel, tt=tt, rc=rc, taps=taps),
        grid=(B, T // tt),
        in_specs=[
            pl.BlockSpec((None, tt, CH), lambda b, t: (b, t, 0)),
            pl.BlockSpec((None, tt, CH), lambda b, t: (b, t, 1)),
            pl.BlockSpec((None, taps - 1, CH), lambda b, t: (b, 0, 0)),
            pl.BlockSpec((taps, CH), lambda b, t: (0, 0)),
            pl.BlockSpec((1, CH), lambda b, t: (0, 0)),
            pl.BlockSpec((1, CH), lambda b, t: (0, 0)),
            pl.BlockSpec((1, CH), lambda b, t: (0, 0)),
        ],
        out_specs=[
            pl.BlockSpec((None, tt, CH), lambda b, t: (b, t, 0)),
            pl.BlockSpec((None, taps - 1, CH), lambda b, t: (b, 0, 0)),
        ],
        out_shape=[jax.ShapeDtypeStruct((B, T, CH), BF16),
                   jax.ShapeDtypeStruct((B, taps - 1, CH), F32)],
        scratch_shapes=[pltpu.VMEM((CONV_HIST + pl.cdiv(tt, rc) * rc, CH), F32), pltpu.VMEM((tt, CH), F32)],
        compiler_params=_cparams(("parallel", "arbitrary")),
        name="conv_a_block",
    )(proj3, proj3, conv_state, conv_w, row(conv_b), row(ln_g), row(ln_b))


def _log_sigmoid_pair(z):
    l = jnp.log(1.0 + jnp.exp(-jnp.abs(z)))
    return jnp.minimum(z, 0.0) - l, jnp.minimum(-z, 0.0) - l


def _suffix_sum(lk, u2_ref):
    hi = lk.astype(BF16)
    lo = (lk - hi.astype(F32)).astype(BF16)
    return jnp.dot(jnp.concatenate([hi, lo], axis=1), u2_ref[...], preferred_element_type=F32)


def _sb_prompt_kernel(bias_ref, q_ref, k_ref, v_ref, u_ref, o_ref, k16_ref, v16_ref, *, tq, tk, scale):
    h = pl.program_id(1)
    i = pl.program_id(2)

    @pl.when(i == 0)
    def _():
        k16_ref[...] = k_ref[...].astype(BF16)
        v16_ref[...] = v_ref[...].astype(BF16)

    bias = bias_ref[h]
    q16 = q_ref[...].astype(BF16)
    nd = tq // tk

    def sweep(j_hi, m, acc, diagonal):
        starts = [j_hi - (n + 1) * tk for n in range(nd)]
        lss, lks = [], []
        for n, j0 in enumerate(starts):
            kb = k16_ref[pl.ds(j0, tk), :]
            z = lax.dot_general(q16, kb, (((1,), (1,)), ((), ())), preferred_element_type=F32) * scale + bias
            ls, lk = _log_sigmoid_pair(z)
            if diagonal:
                qpos = lax.broadcasted_iota(jnp.int32, (tq, tk), 0)
                kpos = lax.broadcasted_iota(jnp.int32, (tq, tk), 1) + (nd - 1 - n) * tk
                vis = kpos < qpos
                ls = jnp.where(vis, ls, -jnp.inf)
                lk = jnp.where(vis, lk, 0.0)
            lss.append(ls)
            lks.append(lk)
        laters = [_suffix_sum(lk, u_ref) for lk in lks]
        for n, j0 in enumerate(starts):
            w = jnp.exp(lss[n] + laters[n] + m)
            acc = acc + jnp.dot(w.astype(BF16), v16_ref[pl.ds(j0, tk), :], preferred_element_type=F32)
            m = m + laters[n][:, 0:1] + lks[n][:, 0:1]
        return m, acc

    base = pl.multiple_of(i * tq, tq)
    m, acc = sweep(base + tq, jnp.zeros((tq, 1), F32), jnp.zeros((tq, o_ref.shape[-1]), F32), True)

    def body(jj, carry):
        return sweep(pl.multiple_of(base - jj * tq, tq), carry[0], carry[1], False)

    m, acc = lax.fori_loop(0, i, body, (m, acc))
    o_ref[...] = acc.astype(o_ref.dtype)


def _suffix_matrix(n):
    r = lax.broadcasted_iota(jnp.int32, (2 * n, n), 0) % n
    c = lax.broadcasted_iota(jnp.int32, (2 * n, n), 1)
    return (r > c).astype(BF16)


def sb_attention_prompt(proj3, sb_bias, tq, tk, heads, q_col, k_col, v_col):
    B, T, _ = proj3.shape
    Dh = LANES
    return pl.pallas_call(
        functools.partial(_sb_prompt_kernel, tq=tq, tk=tk, scale=Dh ** -0.5),
        grid=(B, heads, T // tq),
        in_specs=[
            pl.BlockSpec(memory_space=pltpu.SMEM),
            pl.BlockSpec((None, tq, Dh), lambda b, h, i: (b, i, q_col + h)),
            pl.BlockSpec((None, T, Dh), lambda b, h, i: (b, 0, k_col + h)),
            pl.BlockSpec((None, T, Dh), lambda b, h, i: (b, 0, v_col + h)),
            pl.BlockSpec((2 * tk, tk), lambda b, h, i: (0, 0)),
        ],
        out_specs=pl.BlockSpec((None, tq, Dh), lambda b, h, i: (b, i, h)),
        out_shape=jax.ShapeDtypeStruct((B, T, heads * Dh), BF16),
        scratch_shapes=[pltpu.VMEM((T, Dh), BF16), pltpu.VMEM((T, Dh), BF16)],
        compiler_params=_cparams(("parallel", "parallel", "arbitrary")),
        name="sb_attention_prompt",
    )(sb_bias, proj3, proj3, proj3, _suffix_matrix(tk))


def _sb_sample_kernel(tbl_ref, q_ref, knew_ref, vnew_ref, bias_ref, u_ref, *rest, pp, page, heads, scale):
    page_refs = rest[:pp]
    o_ref, m_ref = rest[pp], rest[pp + 1]
    g = pl.program_id(1)
    R = q_ref.shape[0]
    q16 = q_ref[...].astype(BF16)
    bias = bias_ref[...]
    rhead = lax.broadcasted_iota(jnp.int32, (R, LANES), 0) % heads
    rtime = lax.broadcasted_iota(jnp.int32, (R, LANES), 0) // heads
    ccol = lax.broadcasted_iota(jnp.int32, (R, LANES), 1)
    same_head = (ccol % heads) == rhead

    def sweep(blocks, m, acc, vis):
        nch = blocks[0][0].shape[0] // LANES
        order = [(b, c) for b in range(len(blocks)) for c in reversed(range(nch))]
        lss, lks = {}, {}
        for b, (k2d, _) in enumerate(blocks):
            z = lax.dot_general(q16, k2d.astype(BF16), (((1,), (1,)), ((), ())),
                                preferred_element_type=F32) * scale + bias
            ls, lk = _log_sigmoid_pair(z)
            for c in range(nch):
                lss[b, c] = jnp.where(vis, ls[:, c * LANES:(c + 1) * LANES], -jnp.inf)
                lks[b, c] = jnp.where(vis, lk[:, c * LANES:(c + 1) * LANES], 0.0)
        later = _suffix_sum(jnp.concatenate([lks[bc] for bc in order], axis=0), u_ref)
        ws = {}
        for n_, bc in enumerate(order):
            lat = later[n_ * R:(n_ + 1) * R]
            ws[bc] = jnp.exp(lss[bc] + lat + m).astype(BF16)
            m = m + lat[:, 0:1] + lks[bc][:, 0:1]
        for b, (_, v2d) in enumerate(blocks):
            w = jnp.concatenate([ws[b, c] for c in range(nch)], axis=1)
            acc = acc + jnp.dot(w, v2d.astype(BF16), preferred_element_type=F32)
        return m, acc

    @pl.when(g == 0)
    def _():
        vis_new = same_head & ((ccol // heads) < rtime)
        m, acc = sweep([(knew_ref[...], vnew_ref[...])], jnp.zeros((R, 1), F32),
                       jnp.zeros(o_ref.shape, F32), vis_new)
        m_ref[...] = m
        o_ref[...] = acc

    blocks = [(pr[:, 0].reshape(page * heads, LANES), pr[:, 1].reshape(page * heads, LANES))
              for pr in page_refs]
    m, acc = sweep(blocks, m_ref[...], o_ref[...], same_head)
    m_ref[...] = m
    o_ref[...] = acc


def sb_attention_sample(q, k_new, v_new, cache_kv, layer_idx, page_table, sb_bias, pp):
    B, R, Dh = q.shape
    _, _, page, _, heads, _ = cache_kv.shape
    n_pages = page_table.shape[1]
    bias_col = jnp.tile(sb_bias, R // heads).reshape(R, 1)

    def page_spec(u):
        def imap(b, g, tbl):
            return (layer_idx, tbl[b, n_pages - 1 - (g * pp + u)], 0, 0, 0, 0)
        return pl.BlockSpec((None, None, page, 2, heads, Dh), imap)

    grid_spec = pltpu.PrefetchScalarGridSpec(
        num_scalar_prefetch=1,
        grid=(B, n_pages // pp),
        in_specs=[
            pl.BlockSpec((None, R, Dh), lambda b, g, tbl: (b, 0, 0)),
            pl.BlockSpec((None, LANES, Dh), lambda b, g, tbl: (b, 0, 0)),
            pl.BlockSpec((None, LANES, Dh), lambda b, g, tbl: (b, 0, 0)),
            pl.BlockSpec((R, 1), lambda b, g, tbl: (0, 0)),
            pl.BlockSpec((2 * LANES, LANES), lambda b, g, tbl: (0, 0)),
        ] + [page_spec(u) for u in range(pp)],
        out_specs=pl.BlockSpec((None, R, Dh), lambda b, g, tbl: (b, 0, 0)),
        scratch_shapes=[pltpu.VMEM((R, 1), F32)],
    )
    return pl.pallas_call(
        functools.partial(_sb_sample_kernel, pp=pp, page=page, heads=heads, scale=Dh ** -0.5),
        grid_spec=grid_spec,
        out_shape=jax.ShapeDtypeStruct((B, R, Dh), F32),
        compiler_params=_cparams(("parallel", "arbitrary")),
        name="sb_attention_sample",
    )(page_table, q, k_new, v_new, bias_col, _suffix_matrix(LANES), *([cache_kv] * pp))


GDN_HIST = 8


def _gdn_prep_kernel(x_ref, st_ref, w_ref, o_ref, ns_ref, ext_ref, *, tt, rc, taps, n_q, n_k, scale):
    j = pl.program_id(1)
    t = pl.program_id(2)
    hist = taps - 1
    off = GDN_HIST - hist
    tc = x_ref.shape[-1]

    @pl.when(t == 0)
    def _():
        ext_ref[off:GDN_HIST, :] = st_ref[...]

    @pl.when(t > 0)
    def _():
        ext_ref[0:GDN_HIST, :] = ext_ref[tt:tt + GDN_HIST, :]

    ext_ref[GDN_HIST:GDN_HIST + tt, :] = x_ref[...]
    if tt % rc:
        ext_ref[GDN_HIST + tt:, :] = jnp.zeros((ext_ref.shape[0] - GDN_HIST - tt, tc), F32)

    rows = min(rc, tt)

    def run(normalise, post_scale):
        def chunk(r, carry):
            r0 = pl.multiple_of(r * rc, rc)
            for c in range(tc // LANES):
                cs = slice(c * LANES, (c + 1) * LANES)
                y = _silu(_shifted_taps(ext_ref, w_ref, r0, cs, rc, GDN_HIST, off, taps,
                                        jnp.zeros((rc, LANES), F32)))
                if normalise:
                    y = y * lax.rsqrt(jnp.sum(y * y, axis=-1, keepdims=True) + EPS)
                    if post_scale != 1.0:
                        y = y * post_scale
                o_ref[pl.ds(r0, rows), cs] = y[:rows]
            return carry
        lax.fori_loop(0, pl.cdiv(tt, rc), chunk, 0)

    @pl.when(j < n_q)
    def _():
        run(True, scale)

    @pl.when((j >= n_q) & (j < n_q + n_k))
    def _():
        run(True, 1.0)

    @pl.when(j >= n_q + n_k)
    def _():
        run(False, 1.0)

    @pl.when(t == pl.num_programs(2) - 1)
    def _():
        ns_ref[...] = ext_ref[tt + off:tt + GDN_HIST, :]


def gdn_prep_block(proj3, conv_state, conv_w, kw, tt, rc, tc):
    B, T, _ = proj3.shape
    taps, QKV = conv_w.shape
    return pl.pallas_call(
        functools.partial(_gdn_prep_kernel, tt=tt, rc=rc, taps=taps, n_q=kw // tc, n_k=kw // tc,
                          scale=LANES ** -0.5),
        grid=(B, QKV // tc, T // tt),
        in_specs=[
            pl.BlockSpec((None, tt, tc), lambda b, j, t: (b, t, j)),
            pl.BlockSpec((None, taps - 1, tc), lambda b, j, t: (b, 0, j)),
            pl.BlockSpec((taps, tc), lambda b, j, t: (0, j)),
        ],
        out_specs=[
            pl.BlockSpec((None, tt, tc), lambda b, j, t: (b, t, j)),
            pl.BlockSpec((None, taps - 1, tc), lambda b, j, t: (b, 0, j)),
        ],
        out_shape=[jax.ShapeDtypeStruct((B, T, QKV), F32),
                   jax.ShapeDtypeStruct((B, taps - 1, QKV), F32)],
        scratch_shapes=[pltpu.VMEM((GDN_HIST + pl.cdiv(tt, rc) * rc, tc), F32)],
        compiler_params=_cparams(("parallel", "parallel", "arbitrary")),
        name="gdn_prep_block",
    )(proj3, conv_state, conv_w)


def _split3(x):
    hi = x.astype(BF16)
    r = x - hi.astype(F32)
    mid = r.astype(BF16)
    lo = (r - mid.astype(F32)).astype(BF16)
    return hi, mid, lo


def _gdn_gates_kernel(ba_ref, alog_ref, dtb_ref, tri_ref, o_ref, *, nh, seq_rows, valid_rows):
    ba = ba_ref[...]
    R = ba.shape[0]
    beta = jax.nn.sigmoid(ba[:, :nh])
    x = ba[:, nh:] + dtb_ref[...]
    softplus = jnp.maximum(x, 0.0) + jnp.log1p(jnp.exp(-jnp.abs(x)))
    g = -jnp.exp(alog_ref[...]) * softplus
    if valid_rows < seq_rows:
        row = lax.broadcasted_iota(jnp.int32, (R, nh), 0) % seq_rows
        beta = jnp.where(row < valid_rows, beta, 0.0)
        g = jnp.where(row < valid_rows, g, 0.0)
    tri = tri_ref[...]
    gc = sum(jnp.dot(tri, p, preferred_element_type=F32) for p in _split3(g))
    o_ref[...] = jnp.concatenate([gc, beta, jnp.zeros((R, LANES - 2 * nh), F32)], axis=1)


def gdn_gates(ba, a_log, dt_bias, chunk, seq_rows, valid_rows, tr):
    M, two_nh = ba.shape
    nh = two_nh // 2
    r = lax.broadcasted_iota(jnp.int32, (tr, tr), 0)
    c = lax.broadcasted_iota(jnp.int32, (tr, tr), 1)
    tri = ((r // chunk == c // chunk) & (r >= c)).astype(BF16)
    return pl.pallas_call(
        functools.partial(_gdn_gates_kernel, nh=nh, seq_rows=seq_rows, valid_rows=valid_rows),
        grid=(M // tr,),
        in_specs=[
            pl.BlockSpec((tr, two_nh), lambda i: (i, 0)),
            pl.BlockSpec((1, nh), lambda i: (0, 0)),
            pl.BlockSpec((1, nh), lambda i: (0, 0)),
            pl.BlockSpec((tr, tr), lambda i: (0, 0)),
        ],
        out_specs=pl.BlockSpec((tr, LANES), lambda i: (i, 0)),
        out_shape=jax.ShapeDtypeStruct((M, LANES), F32),
        compiler_params=_cparams(("parallel",)),
        name="gdn_gates",
    )(ba, a_log.reshape(1, nh), dt_bias.reshape(1, nh), tri)


def _mm(a, b):
    return jnp.dot(a.astype(BF16), b.astype(BF16), preferred_element_type=F32)


def _gdn_chunk_kernel(q_ref, k_ref, v_ref, z_ref, gc_ref, bt_ref, s0_ref, ng_ref, o_ref, so_ref, s_ref,
                      *, nc, hk, C):
    t = pl.program_id(2)
    Dh = LANES
    C2 = 2 * C

    @pl.when(t == 0)
    def _():
        s_ref[...] = s0_ref[...]

    ii = lax.broadcasted_iota(jnp.int32, (C, C2), 0)
    ll = lax.broadcasted_iota(jnp.int32, (C, C2), 1)
    jj = ll % C
    left = ll < C
    causal = ii >= jj
    strict = ii > jj
    diag = ii == jj
    eye2 = diag.astype(F32)
    nt = (((1,), (1,)), ((), ()))
    tn = (((0,), (0,)), ((), ()))

    def block_diag(p2):
        return jnp.concatenate([jnp.where(left, p2, 0.0), jnp.where(left, 0.0, p2)], axis=0).astype(BF16)

    units = [(kk, c) for kk in range(hk) for c in range(nc)]
    kq, gcol, bcol, grow2, decay2, attn2, pw, xinv = {}, {}, {}, {}, {}, {}, {}, {}

    for un in units:
        kk, c = un
        rs = slice(c * C, (c + 1) * C)
        ks = slice(kk * Dh, (kk + 1) * Dh)
        kc, qc = k_ref[rs, ks], q_ref[rs, ks]
        kq[un] = (kc, qc)
        k16 = kc.astype(BF16)
        gkk = lax.dot_general(k16, k16, nt, preferred_element_type=F32)
        gqk = lax.dot_general(qc.astype(BF16), k16, nt, preferred_element_type=F32)
        g2 = jnp.concatenate([gc_ref[2 * kk, :, rs], gc_ref[2 * kk + 1, :, rs]], axis=1)
        b2 = jnp.concatenate([bt_ref[2 * kk, :, rs], bt_ref[2 * kk + 1, :, rs]], axis=1)
        gd, bd = jnp.where(diag, g2, 0.0), jnp.where(diag, b2, 0.0)
        gcol[un] = [jnp.sum(gd[:, e * C:(e + 1) * C], axis=1, keepdims=True) for e in range(2)]
        bcol[un] = [jnp.sum(bd[:, e * C:(e + 1) * C], axis=1, keepdims=True) for e in range(2)]
        gcol2 = jnp.where(left, gcol[un][0], gcol[un][1])
        bcol2 = jnp.where(left, bcol[un][0], bcol[un][1])
        grow2[un] = g2
        decay2[un] = jnp.exp(jnp.where(causal, gcol2 - g2, -jnp.inf))
        a2 = jnp.where(strict, jnp.concatenate([gkk, gkk], axis=1) * bcol2 * decay2[un], 0.0)
        attn2[un] = jnp.concatenate([gqk, gqk], axis=1) * decay2[un]
        pw[un] = a2
        xinv[un] = eye2 - a2

    k = 2
    while k < C:
        for un in units:
            pw[un] = jnp.dot(pw[un].astype(BF16), block_diag(pw[un]), preferred_element_type=F32)
        for un in units:
            xinv[un] = xinv[un] + jnp.dot(xinv[un].astype(BF16), block_diag(pw[un]),
                                          preferred_element_type=F32)
        k *= 2

    heads = [(kk, e) for kk in range(hk) for e in range(2)]
    uw, qd, kd, gl = {}, {}, {}, {}
    for un in units:
        kk, c = un
        rs = slice(c * C, (c + 1) * C)
        kc, qc = kq[un]
        for e in range(2):
            vs = slice((2 * kk + e) * Dh, (2 * kk + e + 1) * Dh)
            bc, gcl = bcol[un][e], gcol[un][e]
            egc = jnp.exp(gcl)
            glast = grow2[un][:, e * C + C - 1:e * C + C]
            rhs = jnp.concatenate([v_ref[rs, vs] * bc, kc * (bc * egc)], axis=1)
            uw[kk, e, c] = _mm(xinv[un][:, e * C:(e + 1) * C], rhs)
            qd[kk, e, c] = (qc * egc).astype(BF16)
            kd[kk, e, c] = (kc * jnp.exp(glast - gcl)).astype(BF16)
            gl[kk, e, c] = jnp.exp(glast)

    S = {h: s_ref[2 * h[0] + h[1]] for h in heads}
    for c in range(nc):
        rs = slice(c * C, (c + 1) * C)
        s16 = {h: S[h].astype(BF16) for h in heads}
        v_new = {h: uw[h[0], h[1], c][:, :Dh]
                 - jnp.dot(uw[h[0], h[1], c][:, Dh:].astype(BF16), s16[h], preferred_element_type=F32)
                 for h in heads}
        for h in heads:
            kk, e = h
            S[h] = S[h] * gl[kk, e, c] + lax.dot_general(kd[kk, e, c], v_new[h].astype(BF16), tn,
                                                          preferred_element_type=F32)
        for h in heads:
            kk, e = h
            vs = slice((2 * kk + e) * Dh, (2 * kk + e + 1) * Dh)
            o = (jnp.dot(qd[kk, e, c], s16[h], preferred_element_type=F32)
                 + _mm(attn2[kk, c][:, e * C:(e + 1) * C], v_new[h]))
            o = _rms(o) * ng_ref[...] * _silu(z_ref[rs, vs])
            o_ref[rs, vs] = o.astype(o_ref.dtype)
    for h in heads:
        s_ref[2 * h[0] + h[1]] = S[h]

    @pl.when(t == pl.num_programs(2) - 1)
    def _():
        so_ref[...] = s_ref[...]


def gdn_chunk_block(qkv, proj3, gates_t, s0, norm_g, kw, z_col0, chunk, nc, hk):
    B, T, QKV = qkv.shape
    Dh = LANES
    nh = s0.shape[1]
    kh = kw // Dh
    tcs = nc * chunk
    qw, vw = hk * Dh, 2 * hk * Dh
    return pl.pallas_call(
        functools.partial(_gdn_chunk_kernel, nc=nc, hk=hk, C=chunk),
        grid=(B, kh // hk, T // tcs),
        in_specs=[
            pl.BlockSpec((None, tcs, qw), lambda b, g, t: (b, t, g)),
            pl.BlockSpec((None, tcs, qw), lambda b, g, t: (b, t, kw // qw + g)),
            pl.BlockSpec((None, tcs, vw), lambda b, g, t: (b, t, 2 * kw // vw + g)),
            pl.BlockSpec((None, tcs, vw), lambda b, g, t: (b, t, z_col0 // vw + g)),
            pl.BlockSpec((None, 2 * hk, 1, tcs), lambda b, g, t: (b, g, 0, t)),
            pl.BlockSpec((None, 2 * hk, 1, tcs), lambda b, g, t: (b, nh // (2 * hk) + g, 0, t)),
            pl.BlockSpec((None, 2 * hk, Dh, Dh), lambda b, g, t: (b, g, 0, 0)),
            pl.BlockSpec((1, Dh), lambda b, g, t: (0, 0)),
        ],
        out_specs=[
            pl.BlockSpec((None, tcs, vw), lambda b, g, t: (b, t, g)),
            pl.BlockSpec((None, 2 * hk, Dh, Dh), lambda b, g, t: (b, g, 0, 0)),
        ],
        out_shape=[jax.ShapeDtypeStruct((B, T, nh * Dh), BF16),
                   jax.ShapeDtypeStruct((B, nh, Dh, Dh), F32)],
        scratch_shapes=[pltpu.VMEM((2 * hk, Dh, Dh), F32)],
        compiler_params=_cparams(("parallel", "parallel", "arbitrary")),
        name="gdn_chunk_block",
    )(qkv, qkv, qkv, proj3, gates_t, gates_t, s0, norm_g.reshape(1, Dh))


GDN_CHUNK = 64


class _Cfg:
    def __init__(self, prompt):
        self.prompt = prompt
        if prompt:
            self.tm, self.tm_in, self.conv_tt, self.conv_rc = 512, 1024, 256, 64
            self.gdn_tt, self.gdn_rc, self.gdn_nc, self.gdn_hk = 256, 64, 4, 4
        else:
            self.tm, self.tm_in, self.conv_tt, self.conv_rc = None, None, None, 8
            self.gdn_tt, self.gdn_rc, self.gdn_nc, self.gdn_hk = None, 8, 1, 4


def _conv_sb_mixer(proj3, i, cfg, conv_state, cache_kv, table, P):
    B, T, _ = proj3.shape
    ch = P['conv_a_w'].shape[-1]
    heads, Dh = cache_kv.shape[-2], cache_kv.shape[-1]
    sbw = heads * Dh
    qb, kb, vb = 2 * ch // Dh, (2 * ch + sbw) // Dh, (2 * ch + 2 * sbw) // Dh
    a16, cs = conv_a_block(proj3, conv_state, P['conv_a_w'][i], P['conv_a_b'][i], P['ln_a_g'][i],
                           P['ln_a_b'][i], cfg.conv_tt or T, cfg.conv_rc)
    kv = proj3[:, :, 2 * ch + sbw:].reshape(B, T, 2, heads, Dh)
    if cfg.prompt:
        o16 = sb_attention_prompt(proj3, P['sb_bias'][i], 512, 256, heads, qb, kb, vb)
    else:
        rows = T * heads
        q = proj3[:, :, 2 * ch:2 * ch + sbw].reshape(B, rows, Dh)
        pad = lambda x: jnp.pad(x.reshape(B, rows, Dh), ((0, 0), (0, LANES - rows), (0, 0)))
        o = sb_attention_sample(q, pad(kv[:, :, 0]), pad(kv[:, :, 1]), cache_kv, i, table, P['sb_bias'][i], 8)
        o16 = o.reshape(B, T, sbw).astype(BF16)
    cat = jnp.concatenate([a16, o16], axis=-1)
    return cat.reshape(B * T, ch + sbw), cs, kv


def _gdn_mixer(proj3, ba, i, cfg, conv_state, s0, P):
    B, T, _ = proj3.shape
    nh = s0.shape[1]
    kw = P['w_out_gdn'].shape[1] // 2
    qkv_w = P['conv_c_w'].shape[-1]
    qkv, cs = gdn_prep_block(proj3, conv_state, P['conv_c_w'][i], kw, cfg.gdn_tt or T, cfg.gdn_rc, 1024)
    if T % GDN_CHUNK:
        tp = -(-T // GDN_CHUNK) * GDN_CHUNK
        padt = lambda x: jnp.pad(x, ((0, 0), (0, tp - T), (0, 0)))
        qkv, zsrc, z_col0, ba = padt(qkv), padt(proj3[:, :, qkv_w:]), 0, padt(ba)
    else:
        tp, zsrc, z_col0 = T, proj3, qkv_w
    gates = gdn_gates(ba.reshape(B * tp, 2 * nh), P['a_log'][i], P['dt_bias'][i], GDN_CHUNK, tp, T,
                      min(512, B * tp))
    gates_t = gates.reshape(B, tp, LANES)[:, :, :2 * nh].transpose(0, 2, 1).reshape(B, 2 * nh, 1, tp)
    og, s = gdn_chunk_block(qkv, zsrc, gates_t, s0, P['gdn_norm_g'][i], kw, z_col0, GDN_CHUNK, cfg.gdn_nc,
                            cfg.gdn_hk)
    return og[:, :T].reshape(B * T, -1), cs, s


def _trunk(x, mod, B, T, cfg, cache_kv, table, conv_a_state, conv_c_state, gdn_state, P):
    D = x.shape[-1]
    depth = P['norm_pre'].shape[0]
    tm = cfg.tm or B * T
    tm_in = cfg.tm_in or B * T
    tf = 512
    kv_rows, conv_a_new, conv_c_new, gdn_new = [], [], [], []
    for l in range(depth):
        i = l // 2
        x = ffn_block(x, mod, P['norm_pre'], P['norm_post'], P['wg'], P['wu'], P['wd'], l, 0, 0, tm, tf)
        if l % 2 == 0:
            proj = inproj_block(x, mod, P['norm_pre'], P['w_in_ab'], l, 1, tm_in, 1024)
            cat, cs, kv = _conv_sb_mixer(proj.reshape(B, T, -1), i, cfg, conv_a_state[i], cache_kv, table, P)
            conv_a_new.append(cs)
            kv_rows.append(kv)
            x = outproj_block(x, cat, mod, P['norm_post'], P['w_out_ab'], l, 1, tm, 1024)
        else:
            proj, ba = inproj_block(x, mod, P['norm_pre'], P['w_in_gdn'], l, 1, tm_in, 1024,
                                    w_extra=P['w_in_gdn_ba'])
            og, cs, s = _gdn_mixer(proj.reshape(B, T, -1), ba.reshape(B, T, -1), i, cfg, conv_c_state[i],
                                   gdn_state[i], P)
            conv_c_new.append(cs)
            gdn_new.append(s)
            x = outproj_block(x, og, mod, P['norm_post'], P['w_out_gdn'], l, 1, tm, 1024)
        x = ffn_block(x, mod, P['norm_pre'], P['norm_post'], P['wg'], P['wu'], P['wd'], l, 2, 1, tm, tf)
    return (x.reshape(B, T, D), jnp.stack(kv_rows), jnp.stack(conv_a_new), jnp.stack(conv_c_new),
            jnp.stack(gdn_new))


def kernel(x_prompt, x_sample, cache_kv_sb, state_conv_a, state_conv_c, state_gdn, page_table,
           c_prompt, c_sample, w_ada, b_ada, norm_pre, norm_post, w_ffn_gate, w_ffn_up, w_ffn_down,
           w_in_ab, conv_a_w, conv_a_b, ln_a_g, ln_a_b, w_out_ab, sb_bias,
           w_in_gdn, conv_c_w, a_log, dt_bias, gdn_norm_g, w_out_gdn):
    BP, T, D = x_prompt.shape
    BS, TS, _ = x_sample.shape
    L = w_ada.shape[0]
    n_sb, n_gdn = w_in_ab.shape[0], w_in_gdn.shape[0]
    dt = x_prompt.dtype

    R = -(-(BP + BS) // 8) * 8
    c_all = jnp.concatenate([c_prompt, c_sample, jnp.zeros((R - BP - BS, D), dt)], axis=0)
    mod_all = ada_modulation(c_all, w_ada, b_ada)
    mod_p = _Mod(mod_all[:, :, :BP].reshape(L, 9, BP, 1, D), T, False)
    mod_s = _Mod(jnp.repeat(mod_all[:, :, BP:BP + BS], TS, axis=2), TS, True)

    qkvz = conv_c_w.shape[-1] + w_out_gdn.shape[1]
    P = dict(
        norm_pre=norm_pre.reshape(L, 3, 1, D), norm_post=norm_post.reshape(L, 3, 1, D),
        wg=w_ffn_gate.astype(BF16), wu=w_ffn_up.astype(BF16), wd=w_ffn_down.astype(BF16),
        w_in_ab=w_in_ab.astype(BF16), w_out_ab=w_out_ab.astype(BF16),
        w_in_gdn=w_in_gdn.astype(BF16), w_in_gdn_ba=w_in_gdn[:, :, qkvz:].astype(BF16),
        w_out_gdn=w_out_gdn.astype(BF16),
        conv_a_w=conv_a_w, conv_a_b=conv_a_b, ln_a_g=ln_a_g, ln_a_b=ln_a_b, sb_bias=sb_bias,
        conv_c_w=conv_c_w, a_log=a_log, dt_bias=dt_bias, gdn_norm_g=gdn_norm_g,
    )

    y_p, kv_p, ca_p, cc_p, g_p = _trunk(
        x_prompt.reshape(BP * T, D), mod_p, BP, T, _Cfg(True), cache_kv_sb, None,
        jnp.zeros((n_sb, BP) + state_conv_a.shape[2:], dt), jnp.zeros((n_gdn, BP) + state_conv_c.shape[2:], dt),
        jnp.zeros((n_gdn, BP) + state_gdn.shape[2:], dt), P)
    y_s, kv_s, ca_s, cc_s, g_s = _trunk(
        x_sample.reshape(BS * TS, D), mod_s, BS, TS, _Cfg(False), cache_kv_sb, page_table,
        state_conv_a, state_conv_c, state_gdn, P)
    return (y_p, y_s, kv_p, ca_p, cc_p, g_p, kv_s, ca_s, cc_s, g_s)
```

```python
import functools
import math

import jax
import jax.numpy as jnp
from jax import lax
from jax.experimental import pallas as pl
from jax.experimental.pallas import tpu as pltpu

F32 = jnp.float32
BF16 = jnp.bfloat16
EPS = 1e-6
FFN_RESIDUAL = 0.5
VMEM_LIMIT = 56 * 1024 * 1024


def _cparams(sem):
    return pltpu.CompilerParams(dimension_semantics=sem, vmem_limit_bytes=VMEM_LIMIT)


def _silu(x):
    return x * jax.nn.sigmoid(x)


def _rms(x):
    return x * lax.rsqrt(jnp.mean(x * x, axis=-1, keepdims=True) + EPS)


ROWS = 16
ROWS_UNROLL = 4


def _rows_loop(n_rows, body):
    def step(r, carry):
        body(pl.multiple_of(r * ROWS, ROWS))
        return carry
    n = n_rows // ROWS
    lax.fori_loop(0, n, step, 0, unroll=math.gcd(n, ROWS_UNROLL))


def _mod_rows(ref, r0):
    return ref[...] if ref.shape[0] == 1 else ref[pl.ds(r0, ROWS), :]


def _prenorm_rows(x_ref, sh_ref, sc_ref, gpre_ref, h_ref):
    def body(r0):
        y = _rms(x_ref[pl.ds(r0, ROWS), :]) * gpre_ref[...]
        h_ref[pl.ds(r0, ROWS), :] = (y * (1.0 + _mod_rows(sc_ref, r0)) + _mod_rows(sh_ref, r0)).astype(BF16)
    _rows_loop(x_ref.shape[0], body)


def _postnorm_rows(acc_ref, x_ref, gt_ref, gpost_ref, o_ref, weight):
    def body(r0):
        rs = pl.ds(r0, ROWS)
        y = _rms(acc_ref[rs, :]) * gpost_ref[...]
        o_ref[rs, :] = x_ref[rs, :] + weight * (1.0 + _mod_rows(gt_ref, r0)) * y
    _rows_loop(x_ref.shape[0], body)


def _ada_kernel(c_ref, w_ref, b_ref, o_ref):
    s = _silu(c_ref[...]).astype(BF16)
    o_ref[...] = jnp.dot(s, w_ref[...].astype(BF16), preferred_element_type=F32) + b_ref[...]


def ada_modulation(c_all, w_ada, b_ada):
    L, D, N = w_ada.shape
    R = c_all.shape[0]
    nj = N // D
    return pl.pallas_call(
        _ada_kernel,
        grid=(L, nj),
        in_specs=[
            pl.BlockSpec((R, D), lambda l, j: (0, 0)),
            pl.BlockSpec((None, D, D), lambda l, j: (l, 0, j)),
            pl.BlockSpec((None, 1, D), lambda l, j: (l, 0, j)),
        ],
        out_specs=pl.BlockSpec((None, None, R, D), lambda l, j: (l, j, 0, 0)),
        out_shape=jax.ShapeDtypeStruct((L, nj, R, D), F32),
        compiler_params=_cparams(("parallel", "parallel")),
        name="ada_modulation",
    )(c_all, w_ada, b_ada.reshape(L, 1, N))


class _Mod:
    def __init__(self, arr, rows_per_seq, per_row):
        self.arr = arr
        self.rows_per_seq = rows_per_seq
        self.per_row = per_row

    def spec(self, l, k, tm, ngrid):
        D = self.arr.shape[-1]
        if self.per_row:
            def imap(i, *_):
                return (l, k, i, 0)
            return pl.BlockSpec((None, None, tm, D), imap)
        rps = self.rows_per_seq

        def imap(i, *_):
            return (l, k, (i * tm) // rps, 0, 0)
        return pl.BlockSpec((None, None, None, 1, D), imap)


def _vec_spec(l, s, D):
    return pl.BlockSpec((None, None, 1, D), lambda i, *_: (l, s, 0, 0))


def _ffn_kernel(x_ref, sh_ref, sc_ref, gt_ref, gpre_ref, gpost_ref, wg_ref, wu_ref, wd_ref,
                o_ref, h_ref, acc_ref, *, weight):
    f = pl.program_id(1)

    @pl.when(f == 0)
    def _():
        y = _rms(x_ref[...]) * gpre_ref[...]
        h_ref[...] = (y * (1.0 + sc_ref[...]) + sh_ref[...]).astype(BF16)
        acc_ref[...] = jnp.zeros_like(acc_ref)

    h = h_ref[...]
    g = jnp.dot(h, wg_ref[...], preferred_element_type=F32)
    u = jnp.dot(h, wu_ref[...], preferred_element_type=F32)
    a = (_silu(g) * u).astype(BF16)
    acc_ref[...] += jnp.dot(a, wd_ref[...], preferred_element_type=F32)

    @pl.when(f == pl.num_programs(1) - 1)
    def _():
        y = _rms(acc_ref[...]) * gpost_ref[...]
        o_ref[...] = x_ref[...] + weight * (1.0 + gt_ref[...]) * y


def ffn_block(x, mod, norm_pre, norm_post, wg, wu, wd, l, s, half, tm, tf):
    M, D = x.shape
    F = wg.shape[-1]
    grid = (M // tm, F // tf)
    k0 = 3 * s
    return pl.pallas_call(
        functools.partial(_ffn_kernel, weight=FFN_RESIDUAL),
        grid=grid,
        in_specs=[
            pl.BlockSpec((tm, D), lambda i, f: (i, 0)),
            mod.spec(l, k0 + 0, tm, grid), mod.spec(l, k0 + 1, tm, grid), mod.spec(l, k0 + 2, tm, grid),
            _vec_spec(l, s, D), _vec_spec(l, s, D),
            pl.BlockSpec((None, None, D, tf), lambda i, f: (l, half, 0, f)),
            pl.BlockSpec((None, None, D, tf), lambda i, f: (l, half, 0, f)),
            pl.BlockSpec((None, None, tf, D), lambda i, f: (l, half, f, 0)),
        ],
        out_specs=pl.BlockSpec((tm, D), lambda i, f: (i, 0)),
        out_shape=jax.ShapeDtypeStruct((M, D), F32),
        scratch_shapes=[pltpu.VMEM((tm, D), BF16), pltpu.VMEM((tm, D), F32)],
        compiler_params=_cparams(("parallel", "arbitrary")),
        name="ffn_block",
    )(x, mod.arr, mod.arr, mod.arr, norm_pre, norm_post, wg, wu, wd)


def _inproj_kernel(x_ref, sh_ref, sc_ref, gpre_ref, w_ref, *rest, has_extra):
    if has_extra:
        we_ref, o_ref, oe_ref, h_ref = rest
    else:
        o_ref, h_ref = rest
    j = pl.program_id(1)

    @pl.when(j == 0)
    def _():
        _prenorm_rows(x_ref, sh_ref, sc_ref, gpre_ref, h_ref)
        if has_extra:
            oe_ref[...] = jnp.dot(h_ref[...], we_ref[...], preferred_element_type=F32)

    o_ref[...] = jnp.dot(h_ref[...], w_ref[...], preferred_element_type=F32)


def inproj_block(x, mod, norm_pre, w, l, s, tm, tn, w_extra=None):
    M, D = x.shape
    N = w.shape[-1] // tn * tn
    i_kind = l // 2
    grid = (M // tm, N // tn)
    k0 = 3 * s
    in_specs = [
        pl.BlockSpec((tm, D), lambda i, j: (i, 0)),
        mod.spec(l, k0 + 0, tm, grid), mod.spec(l, k0 + 1, tm, grid),
        _vec_spec(l, s, D),
        pl.BlockSpec((None, D, tn), lambda i, j: (i_kind, 0, j)),
    ]
    args = [x, mod.arr, mod.arr, norm_pre, w]
    out_specs = [pl.BlockSpec((tm, tn), lambda i, j: (i, j))]
    out_shape = [jax.ShapeDtypeStruct((M, N), F32)]
    if w_extra is not None:
        ne = w_extra.shape[-1]
        in_specs.append(pl.BlockSpec((None, D, ne), lambda i, j: (i_kind, 0, 0)))
        args.append(w_extra)
        out_specs.append(pl.BlockSpec((tm, ne), lambda i, j: (i, 0)))
        out_shape.append(jax.ShapeDtypeStruct((M, ne), F32))
    res = pl.pallas_call(
        functools.partial(_inproj_kernel, has_extra=w_extra is not None),
        grid=grid,
        in_specs=in_specs,
        out_specs=out_specs,
        out_shape=out_shape,
        scratch_shapes=[pltpu.VMEM((tm, D), BF16)],
        compiler_params=_cparams(("parallel", "arbitrary")),
        name="inproj_block",
    )(*args)
    return res if w_extra is not None else res[0]


def _outproj_kernel(x_ref, gt_ref, gpost_ref, w_ref, *rest, starts):
    a_refs, (o_ref, acc_ref) = rest[:-2], rest[-2:]
    k = pl.program_id(1)

    @pl.when(k == 0)
    def _():
        acc_ref[...] = jnp.zeros_like(acc_ref)

    for p, a_ref in enumerate(a_refs):
        @pl.when((k >= starts[p]) & (k < starts[p + 1]))
        def _():
            acc_ref[...] += jnp.dot(a_ref[...], w_ref[...], preferred_element_type=F32)

    @pl.when(k == pl.num_programs(1) - 1)
    def _():
        _postnorm_rows(acc_ref, x_ref, gt_ref, gpost_ref, o_ref, 1.0)


def outproj_block(x, a_parts, mod, norm_post, w, l, s, tm, tk):
    M, D = x.shape
    i_kind = l // 2
    starts = [0]
    for a in a_parts:
        starts.append(starts[-1] + a.shape[-1] // tk)
    grid = (M // tm, starts[-1])

    def a_spec(p):
        lo, n = starts[p], starts[p + 1] - starts[p]
        return pl.BlockSpec((tm, tk), lambda i, k: (i, jnp.clip(k - lo, 0, n - 1)))

    return pl.pallas_call(
        functools.partial(_outproj_kernel, starts=tuple(starts)),
        grid=grid,
        in_specs=[
            pl.BlockSpec((tm, D), lambda i, k: (i, 0)),
            mod.spec(l, 3 * s + 2, tm, grid),
            _vec_spec(l, s, D),
            pl.BlockSpec((None, tk, D), lambda i, k: (i_kind, k, 0)),
        ] + [a_spec(p) for p in range(len(a_parts))],
        out_specs=pl.BlockSpec((tm, D), lambda i, k: (i, 0)),
        out_shape=jax.ShapeDtypeStruct((M, D), F32),
        scratch_shapes=[pltpu.VMEM((tm, D), F32)],
        compiler_params=_cparams(("parallel", "arbitrary")),
        name="outproj_block",
    )(x, mod.arr, norm_post, w, *a_parts)


CONV_HIST = 32
LANES = 128


def _shifted_taps(ext_ref, w_ref, r0, cs, rc, hist_pad, off, taps, acc):
    n = rc + hist_pad
    win = ext_ref[pl.ds(r0, n), cs]
    for phase in range(8):
        wb = None
        for a8 in range(hist_pad // 8 + 1):
            j = 8 * a8 + phase - off
            if 0 <= j < taps:
                if wb is None:
                    wb = win if phase == 0 else pltpu.roll(win, n - phase, 0)
                acc = acc + w_ref[j:j + 1, cs] * wb[8 * a8:8 * a8 + rc]
    return acc


def _conv_a_kernel(val_ref, gate_ref, st_ref, w_ref, b_ref, lg_ref, lb_ref, a_ref, ns_ref,
                   ext_ref, y_ref, *, tt, rc, taps):
    t = pl.program_id(1)
    hist = taps - 1
    off = CONV_HIST - hist
    ch = val_ref.shape[-1]

    @pl.when(t == 0)
    def _():
        ext_ref[off:CONV_HIST, :] = st_ref[...]

    @pl.when(t > 0)
    def _():
        ext_ref[0:CONV_HIST, :] = ext_ref[tt:tt + CONV_HIST, :]

    ext_ref[CONV_HIST:CONV_HIST + tt, :] = val_ref[...] * jax.nn.sigmoid(gate_ref[...])
    if tt % rc:
        ext_ref[CONV_HIST + tt:, :] = jnp.zeros((ext_ref.shape[0] - CONV_HIST - tt, ch), F32)

    def chunk(r, carry):
        r0 = pl.multiple_of(r * rc, rc)
        for c in range(ch // LANES):
            cs = slice(c * LANES, (c + 1) * LANES)
            acc = _shifted_taps(ext_ref, w_ref, r0, cs, rc, CONV_HIST, off, taps,
                                jnp.broadcast_to(b_ref[:, cs], (rc, LANES)))
            rows = min(rc, tt)
            y_ref[pl.ds(r0, rows), cs] = acc[:rows]
        return carry

    lax.fori_loop(0, pl.cdiv(tt, rc), chunk, 0)

    y = y_ref[...]
    mu = jnp.mean(y, axis=-1, keepdims=True)
    d = y - mu
    var = jnp.mean(d * d, axis=-1, keepdims=True)
    a_ref[...] = _silu(d * lax.rsqrt(var + EPS) * lg_ref[...] + lb_ref[...]).astype(a_ref.dtype)

    @pl.when(t == pl.num_programs(1) - 1)
    def _():
        ns_ref[...] = ext_ref[tt + off:tt + CONV_HIST, :]


def conv_a_block(proj3, conv_state, conv_w, conv_b, ln_g, ln_b, tt, rc):
    B, T, _ = proj3.shape
    taps, CH = conv_w.shape
    row = lambda v: v.reshape(1, CH)
    return pl.pallas_call(
        functools.partial(_conv_a_kernel, tt=tt, rc=rc, taps=taps),
        grid=(B, T // tt),
        in_specs=[
            pl.BlockSpec((None, tt, CH), lambda b, t: (b, t, 0)),
            pl.BlockSpec((None, tt, CH), lambda b, t: (b, t, 1)),
            pl.BlockSpec((None, taps - 1, CH), lambda b, t: (b, 0, 0)),
            pl.BlockSpec((taps, CH), lambda b, t: (0, 0)),
            pl.BlockSpec((1, CH), lambda b, t: (0, 0)),
            pl.BlockSpec((1, CH), lambda b, t: (0, 0)),
            pl.BlockSpec((1, CH), lambda b, t: (0, 0)),
        ],
        out_specs=[
            pl.BlockSpec((None, tt, CH), lambda b, t: (b, t, 0)),
            pl.BlockSpec((None, taps - 1, CH), lambda b, t: (b, 0, 0)),
        ],
        out_shape=[jax.ShapeDtypeStruct((B, T, CH), BF16),
                   jax.ShapeDtypeStruct((B, taps - 1, CH), F32)],
        scratch_shapes=[pltpu.VMEM((CONV_HIST + pl.cdiv(tt, rc) * rc, CH), F32), pltpu.VMEM((tt, CH), F32)],
        compiler_params=_cparams(("parallel", "arbitrary")),
        name="conv_a_block",
    )(proj3, proj3, conv_state, conv_w, row(conv_b), row(ln_g), row(ln_b))


def _log_sigmoid_pair(z):
    l = jnp.log(1.0 + jnp.exp(-jnp.abs(z)))
    return jnp.minimum(z, 0.0) - l, jnp.minimum(-z, 0.0) - l


def _suffix_sum(lk, u2_ref):
    hi = lk.astype(BF16)
    lo = (lk - hi.astype(F32)).astype(BF16)
    return jnp.dot(jnp.concatenate([hi, lo], axis=1), u2_ref[...], preferred_element_type=F32)


def _sb_prompt_kernel(bias_ref, q_ref, k_ref, v_ref, u_ref, o_ref, k16_ref, v16_ref, *, tq, tk, scale):
    h = pl.program_id(1)
    i = pl.program_id(2)

    @pl.when(i == 0)
    def _():
        k16_ref[...] = k_ref[...].astype(BF16)
        v16_ref[...] = v_ref[...].astype(BF16)

    bias = bias_ref[h]
    q16 = q_ref[...].astype(BF16)
    nd = tq // tk

    def sweep(j_hi, m, acc, diagonal):
        starts = [j_hi - (n + 1) * tk for n in range(nd)]
        lss, lks = [], []
        for n, j0 in enumerate(starts):
            kb = k16_ref[pl.ds(j0, tk), :]
            z = lax.dot_general(q16, kb, (((1,), (1,)), ((), ())), preferred_element_type=F32) * scale + bias
            ls, lk = _log_sigmoid_pair(z)
            if diagonal:
                qpos = lax.broadcasted_iota(jnp.int32, (tq, tk), 0)
                kpos = lax.broadcasted_iota(jnp.int32, (tq, tk), 1) + (nd - 1 - n) * tk
                vis = kpos < qpos
                ls = jnp.where(vis, ls, -jnp.inf)
                lk = jnp.where(vis, lk, 0.0)
            lss.append(ls)
            lks.append(lk)
        laters = [_suffix_sum(lk, u_ref) for lk in lks]
        for n, j0 in enumerate(starts):
            w = jnp.exp(lss[n] + laters[n] + m)
            acc = acc + jnp.dot(w.astype(BF16), v16_ref[pl.ds(j0, tk), :], preferred_element_type=F32)
            m = m + laters[n][:, 0:1] + lks[n][:, 0:1]
        return m, acc

    base = pl.multiple_of(i * tq, tq)
    m, acc = sweep(base + tq, jnp.zeros((tq, 1), F32), jnp.zeros((tq, o_ref.shape[-1]), F32), True)

    def body(jj, carry):
        return sweep(pl.multiple_of(base - jj * tq, tq), carry[0], carry[1], False)

    m, acc = lax.fori_loop(0, i, body, (m, acc))
    o_ref[...] = acc.astype(o_ref.dtype)


def _suffix_matrix(n):
    r = lax.broadcasted_iota(jnp.int32, (2 * n, n), 0) % n
    c = lax.broadcasted_iota(jnp.int32, (2 * n, n), 1)
    return (r > c).astype(BF16)


def sb_attention_prompt(proj3, sb_bias, tq, tk, heads, q_col, k_col, v_col):
    B, T, _ = proj3.shape
    Dh = LANES
    return pl.pallas_call(
        functools.partial(_sb_prompt_kernel, tq=tq, tk=tk, scale=Dh ** -0.5),
        grid=(B, heads, T // tq),
        in_specs=[
            pl.BlockSpec(memory_space=pltpu.SMEM),
            pl.BlockSpec((None, tq, Dh), lambda b, h, i: (b, i, q_col + h)),
            pl.BlockSpec((None, T, Dh), lambda b, h, i: (b, 0, k_col + h)),
            pl.BlockSpec((None, T, Dh), lambda b, h, i: (b, 0, v_col + h)),
            pl.BlockSpec((2 * tk, tk), lambda b, h, i: (0, 0)),
        ],
        out_specs=pl.BlockSpec((None, tq, Dh), lambda b, h, i: (b, i, h)),
        out_shape=jax.ShapeDtypeStruct((B, T, heads * Dh), BF16),
        scratch_shapes=[pltpu.VMEM((T, Dh), BF16), pltpu.VMEM((T, Dh), BF16)],
        compiler_params=_cparams(("parallel", "parallel", "arbitrary")),
        name="sb_attention_prompt",
    )(sb_bias, proj3, proj3, proj3, _suffix_matrix(tk))


def _sb_sample_kernel(tbl_ref, q_ref, knew_ref, vnew_ref, bias_ref, u_ref, *rest, pp, page, heads, scale):
    page_refs = rest[:pp]
    o_ref, m_ref = rest[pp], rest[pp + 1]
    g = pl.program_id(1)
    R = q_ref.shape[0]
    q16 = q_ref[...].astype(BF16)
    bias = bias_ref[...]
    rhead = lax.broadcasted_iota(jnp.int32, (R, LANES), 0) % heads
    rtime = lax.broadcasted_iota(jnp.int32, (R, LANES), 0) // heads
    ccol = lax.broadcasted_iota(jnp.int32, (R, LANES), 1)
    same_head = (ccol % heads) == rhead

    def sweep(blocks, m, acc, vis):
        nch = blocks[0][0].shape[0] // LANES
        order = [(b, c) for b in range(len(blocks)) for c in reversed(range(nch))]
        lss, lks = {}, {}
        for b, (k2d, _) in enumerate(blocks):
            z = lax.dot_general(q16, k2d.astype(BF16), (((1,), (1,)), ((), ())),
                                preferred_element_type=F32) * scale + bias
            ls, lk = _log_sigmoid_pair(z)
            for c in range(nch):
                lss[b, c] = jnp.where(vis, ls[:, c * LANES:(c + 1) * LANES], -jnp.inf)
                lks[b, c] = jnp.where(vis, lk[:, c * LANES:(c + 1) * LANES], 0.0)
        later = _suffix_sum(jnp.concatenate([lks[bc] for bc in order], axis=0), u_ref)
        ws = {}
        for n_, bc in enumerate(order):
            lat = later[n_ * R:(n_ + 1) * R]
            ws[bc] = jnp.exp(lss[bc] + lat + m).astype(BF16)
            m = m + lat[:, 0:1] + lks[bc][:, 0:1]
        for b, (_, v2d) in enumerate(blocks):
            w = jnp.concatenate([ws[b, c] for c in range(nch)], axis=1)
            acc = acc + jnp.dot(w, v2d.astype(BF16), preferred_element_type=F32)
        return m, acc

    @pl.when(g == 0)
    def _():
        vis_new = same_head & ((ccol // heads) < rtime)
        m, acc = sweep([(knew_ref[...], vnew_ref[...])], jnp.zeros((R, 1), F32),
                       jnp.zeros(o_ref.shape, F32), vis_new)
        m_ref[...] = m
        o_ref[...] = acc

    blocks = [(pr[:, 0].reshape(page * heads, LANES), pr[:, 1].reshape(page * heads, LANES))
              for pr in page_refs]
    m, acc = sweep(blocks, m_ref[...], o_ref[...], same_head)
    m_ref[...] = m
    o_ref[...] = acc


def sb_attention_sample(q, k_new, v_new, cache_kv, layer_idx, page_table, sb_bias, pp):
    B, R, Dh = q.shape
    _, _, page, _, heads, _ = cache_kv.shape
    n_pages = page_table.shape[1]
    bias_col = jnp.tile(sb_bias, R // heads).reshape(R, 1)

    def page_spec(u):
        def imap(b, g, tbl):
            return (layer_idx, tbl[b, n_pages - 1 - (g * pp + u)], 0, 0, 0, 0)
        return pl.BlockSpec((None, None, page, 2, heads, Dh), imap)

    grid_spec = pltpu.PrefetchScalarGridSpec(
        num_scalar_prefetch=1,
        grid=(B, n_pages // pp),
        in_specs=[
            pl.BlockSpec((None, R, Dh), lambda b, g, tbl: (b, 0, 0)),
            pl.BlockSpec((None, LANES, Dh), lambda b, g, tbl: (b, 0, 0)),
            pl.BlockSpec((None, LANES, Dh), lambda b, g, tbl: (b, 0, 0)),
            pl.BlockSpec((R, 1), lambda b, g, tbl: (0, 0)),
            pl.BlockSpec((2 * LANES, LANES), lambda b, g, tbl: (0, 0)),
        ] + [page_spec(u) for u in range(pp)],
        out_specs=pl.BlockSpec((None, R, Dh), lambda b, g, tbl: (b, 0, 0)),
        scratch_shapes=[pltpu.VMEM((R, 1), F32)],
    )
    return pl.pallas_call(
        functools.partial(_sb_sample_kernel, pp=pp, page=page, heads=heads, scale=Dh ** -0.5),
        grid_spec=grid_spec,
        out_shape=jax.ShapeDtypeStruct((B, R, Dh), F32),
        compiler_params=_cparams(("parallel", "arbitrary")),
        name="sb_attention_sample",
    )(page_table, q, k_new, v_new, bias_col, _suffix_matrix(LANES), *([cache_kv] * pp))


GDN_HIST = 8


def _gdn_prep_kernel(x_ref, st_ref, w_ref, o_ref, ns_ref, ext_ref, *, tt, rc, taps, n_q, n_k, scale):
    j = pl.program_id(1)
    t = pl.program_id(2)
    hist = taps - 1
    off = GDN_HIST - hist
    tc = x_ref.shape[-1]

    @pl.when(t == 0)
    def _():
        ext_ref[off:GDN_HIST, :] = st_ref[...]

    @pl.when(t > 0)
    def _():
        ext_ref[0:GDN_HIST, :] = ext_ref[tt:tt + GDN_HIST, :]

    ext_ref[GDN_HIST:GDN_HIST + tt, :] = x_ref[...]
    if tt % rc:
        ext_ref[GDN_HIST + tt:, :] = jnp.zeros((ext_ref.shape[0] - GDN_HIST - tt, tc), F32)

    rows = min(rc, tt)

    def run(normalise, post_scale):
        def chunk(r, carry):
            r0 = pl.multiple_of(r * rc, rc)
            for c in range(tc // LANES):
                cs = slice(c * LANES, (c + 1) * LANES)
                y = _silu(_shifted_taps(ext_ref, w_ref, r0, cs, rc, GDN_HIST, off, taps,
                                        jnp.zeros((rc, LANES), F32)))
                if normalise:
                    y = y * lax.rsqrt(jnp.sum(y * y, axis=-1, keepdims=True) + EPS)
                    if post_scale != 1.0:
                        y = y * post_scale
                o_ref[pl.ds(r0, rows), cs] = y[:rows]
            return carry
        lax.fori_loop(0, pl.cdiv(tt, rc), chunk, 0)

    @pl.when(j < n_q)
    def _():
        run(True, scale)

    @pl.when((j >= n_q) & (j < n_q + n_k))
    def _():
        run(True, 1.0)

    @pl.when(j >= n_q + n_k)
    def _():
        run(False, 1.0)

    @pl.when(t == pl.num_programs(2) - 1)
    def _():
        ns_ref[...] = ext_ref[tt + off:tt + GDN_HIST, :]


def gdn_prep_block(proj3, conv_state, conv_w, kw, tt, rc, tc):
    B, T, _ = proj3.shape
    taps, QKV = conv_w.shape
    return pl.pallas_call(
        functools.partial(_gdn_prep_kernel, tt=tt, rc=rc, taps=taps, n_q=kw // tc, n_k=kw // tc,
                          scale=LANES ** -0.5),
        grid=(B, QKV // tc, T // tt),
        in_specs=[
            pl.BlockSpec((None, tt, tc), lambda b, j, t: (b, t, j)),
            pl.BlockSpec((None, taps - 1, tc), lambda b, j, t: (b, 0, j)),
            pl.BlockSpec((taps, tc), lambda b, j, t: (0, j)),
        ],
        out_specs=[
            pl.BlockSpec((None, tt, tc), lambda b, j, t: (b, t, j)),
            pl.BlockSpec((None, taps - 1, tc), lambda b, j, t: (b, 0, j)),
        ],
        out_shape=[jax.ShapeDtypeStruct((B, T, QKV), F32),
                   jax.ShapeDtypeStruct((B, taps - 1, QKV), F32)],
        scratch_shapes=[pltpu.VMEM((GDN_HIST + pl.cdiv(tt, rc) * rc, tc), F32)],
        compiler_params=_cparams(("parallel", "parallel", "arbitrary")),
        name="gdn_prep_block",
    )(proj3, conv_state, conv_w)


def _split3(x):
    hi = x.astype(BF16)
    r = x - hi.astype(F32)
    mid = r.astype(BF16)
    lo = (r - mid.astype(F32)).astype(BF16)
    return hi, mid, lo


def _gdn_gates_kernel(ba_ref, alog_ref, dtb_ref, tri_ref, o_ref, *, nh, seq_rows, valid_rows):
    ba = ba_ref[...]
    R = ba.shape[0]
    beta = jax.nn.sigmoid(ba[:, :nh])
    x = ba[:, nh:] + dtb_ref[...]
    softplus = jnp.maximum(x, 0.0) + jnp.log1p(jnp.exp(-jnp.abs(x)))
    g = -jnp.exp(alog_ref[...]) * softplus
    if valid_rows < seq_rows:
        row = lax.broadcasted_iota(jnp.int32, (R, nh), 0) % seq_rows
        beta = jnp.where(row < valid_rows, beta, 0.0)
        g = jnp.where(row < valid_rows, g, 0.0)
    tri = tri_ref[...]
    gc = sum(jnp.dot(tri, p, preferred_element_type=F32) for p in _split3(g))
    o_ref[...] = jnp.concatenate([gc, beta, jnp.zeros((R, LANES - 2 * nh), F32)], axis=1)


def gdn_gates(ba, a_log, dt_bias, chunk, seq_rows, valid_rows, tr):
    M, two_nh = ba.shape
    nh = two_nh // 2
    r = lax.broadcasted_iota(jnp.int32, (tr, tr), 0)
    c = lax.broadcasted_iota(jnp.int32, (tr, tr), 1)
    tri = ((r // chunk == c // chunk) & (r >= c)).astype(BF16)
    return pl.pallas_call(
        functools.partial(_gdn_gates_kernel, nh=nh, seq_rows=seq_rows, valid_rows=valid_rows),
        grid=(M // tr,),
        in_specs=[
            pl.BlockSpec((tr, two_nh), lambda i: (i, 0)),
            pl.BlockSpec((1, nh), lambda i: (0, 0)),
            pl.BlockSpec((1, nh), lambda i: (0, 0)),
            pl.BlockSpec((tr, tr), lambda i: (0, 0)),
        ],
        out_specs=pl.BlockSpec((tr, LANES), lambda i: (i, 0)),
        out_shape=jax.ShapeDtypeStruct((M, LANES), F32),
        compiler_params=_cparams(("parallel",)),
        name="gdn_gates",
    )(ba, a_log.reshape(1, nh), dt_bias.reshape(1, nh), tri)


def _gdn_chunk_kernel(q_ref, k_ref, v_ref, z_ref, gc_ref, bt_ref, s0_ref, ng_ref, o_ref, so_ref, s_ref,
                      *, nc, hk, C):
    t = pl.program_id(2)
    Dh = LANES
    C2 = 2 * C

    @pl.when(t == 0)
    def _():
        s_ref[...] = s0_ref[...]

    ii = lax.broadcasted_iota(jnp.int32, (C, C2), 0)
    ll = lax.broadcasted_iota(jnp.int32, (C, C2), 1)
    jj = ll % C
    left = ll < C
    causal = ii >= jj
    strict = ii > jj
    diag = ii == jj
    diag_of = [ii == ll, ii + C == ll]
    half_of = [left, ll >= C]
    eye2 = diag.astype(F32)
    nt = (((1,), (1,)), ((), ()))
    tn = (((0,), (0,)), ((), ()))

    def block_diag(p2):
        return jnp.concatenate([jnp.where(left, p2, 0.0), jnp.where(left, 0.0, p2)], axis=0).astype(BF16)

    units = [(kk, c) for kk in range(hk) for c in range(nc)]
    kq, gcol, bcol, grow2, decay2, attn2, pw, xinv = {}, {}, {}, {}, {}, {}, {}, {}

    for un in units:
        kk, c = un
        rs = slice(c * C, (c + 1) * C)
        ks = slice(kk * Dh, (kk + 1) * Dh)
        kc, qc = k_ref[rs, ks], q_ref[rs, ks]
        kq[un] = (kc, qc)
        k16 = kc.astype(BF16)
        k16x2 = jnp.concatenate([k16, k16], axis=0)
        gkk2 = lax.dot_general(k16, k16x2, nt, preferred_element_type=F32)
        gqk2 = lax.dot_general(qc.astype(BF16), k16x2, nt, preferred_element_type=F32)
        g2 = jnp.concatenate([gc_ref[2 * kk, :, rs], gc_ref[2 * kk + 1, :, rs]], axis=1)
        b2 = jnp.concatenate([bt_ref[2 * kk, :, rs], bt_ref[2 * kk + 1, :, rs]], axis=1)
        gcol[un] = [jnp.sum(jnp.where(dm, g2, 0.0), axis=1, keepdims=True) for dm in diag_of]
        bcol[un] = [jnp.sum(jnp.where(dm, b2, 0.0), axis=1, keepdims=True) for dm in diag_of]
        gcol2 = jnp.where(left, gcol[un][0], gcol[un][1])
        bcol2 = jnp.where(left, bcol[un][0], bcol[un][1])
        grow2[un] = g2
        decay2[un] = jnp.exp(jnp.where(causal, gcol2 - g2, -jnp.inf))
        a2 = jnp.where(strict, gkk2 * bcol2 * decay2[un], 0.0)
        attn2[un] = gqk2 * decay2[un]
        pw[un] = a2
        xinv[un] = eye2 - a2

    k = 2
    while k < C:
        for un in units:
            pw[un] = jnp.dot(pw[un].astype(BF16), block_diag(pw[un]), preferred_element_type=F32)
        for un in units:
            xinv[un] = xinv[un] + jnp.dot(xinv[un].astype(BF16), block_diag(pw[un]),
                                          preferred_element_type=F32)
        k *= 2

    heads = [(kk, e) for kk in range(hk) for e in range(2)]
    uw, qd, kd, gl = {}, {}, {}, {}
    for un in units:
        kk, c = un
        rs = slice(c * C, (c + 1) * C)
        kc, qc = kq[un]
        rhs = []
        for e in range(2):
            vs = slice((2 * kk + e) * Dh, (2 * kk + e + 1) * Dh)
            bc, gcl = bcol[un][e], gcol[un][e]
            egc = jnp.exp(gcl)
            glast = grow2[un][:, e * C + C - 1:e * C + C]
            rhs.append(jnp.concatenate([v_ref[rs, vs] * bc, kc * (bc * egc)], axis=1).astype(BF16))
            qd[kk, e, c] = (qc * egc).astype(BF16)
            kd[kk, e, c] = (kc * jnp.exp(glast - gcl)).astype(BF16)
            gl[kk, e, c] = jnp.exp(glast)
        rhs2 = jnp.concatenate(rhs, axis=0)
        for e in range(2):
            uw[kk, e, c] = jnp.dot(jnp.where(half_of[e], xinv[un], 0.0).astype(BF16), rhs2,
                                   preferred_element_type=F32)
            attn2[kk, e, c] = jnp.where(half_of[e], attn2[un], 0.0).astype(BF16)

    S = {h: s_ref[2 * h[0] + h[1]] for h in heads}
    for c in range(nc):
        rs = slice(c * C, (c + 1) * C)
        s16 = {h: S[h].astype(BF16) for h in heads}
        v_new = {h: uw[h[0], h[1], c][:, :Dh]
                 - jnp.dot(uw[h[0], h[1], c][:, Dh:].astype(BF16), s16[h], preferred_element_type=F32)
                 for h in heads}
        for h in heads:
            kk, e = h
            S[h] = S[h] * gl[kk, e, c] + lax.dot_general(kd[kk, e, c], v_new[h].astype(BF16), tn,
                                                          preferred_element_type=F32)
        vn2 = {kk: jnp.concatenate([v_new[kk, 0], v_new[kk, 1]], axis=0).astype(BF16) for kk in range(hk)}
        for h in heads:
            kk, e = h
            vs = slice((2 * kk + e) * Dh, (2 * kk + e + 1) * Dh)
            o = (jnp.dot(qd[kk, e, c], s16[h], preferred_element_type=F32)
                 + jnp.dot(attn2[kk, e, c], vn2[kk], preferred_element_type=F32))
            o = _rms(o) * ng_ref[...] * _silu(z_ref[rs, vs])
            o_ref[rs, vs] = o.astype(o_ref.dtype)
    for h in heads:
        s_ref[2 * h[0] + h[1]] = S[h]

    @pl.when(t == pl.num_programs(2) - 1)
    def _():
        so_ref[...] = s_ref[...]


def gdn_chunk_block(qkv, proj3, gates_t, s0, norm_g, kw, z_col0, chunk, nc, hk):
    B, T, QKV = qkv.shape
    Dh = LANES
    nh = s0.shape[1]
    kh = kw // Dh
    tcs = nc * chunk
    qw, vw = hk * Dh, 2 * hk * Dh
    return pl.pallas_call(
        functools.partial(_gdn_chunk_kernel, nc=nc, hk=hk, C=chunk),
        grid=(B, kh // hk, T // tcs),
        in_specs=[
            pl.BlockSpec((None, tcs, qw), lambda b, g, t: (b, t, g)),
            pl.BlockSpec((None, tcs, qw), lambda b, g, t: (b, t, kw // qw + g)),
            pl.BlockSpec((None, tcs, vw), lambda b, g, t: (b, t, 2 * kw // vw + g)),
            pl.BlockSpec((None, tcs, vw), lambda b, g, t: (b, t, z_col0 // vw + g)),
            pl.BlockSpec((None, 2 * hk, 1, tcs), lambda b, g, t: (b, g, 0, t)),
            pl.BlockSpec((None, 2 * hk, 1, tcs), lambda b, g, t: (b, nh // (2 * hk) + g, 0, t)),
            pl.BlockSpec((None, 2 * hk, Dh, Dh), lambda b, g, t: (b, g, 0, 0)),
            pl.BlockSpec((1, Dh), lambda b, g, t: (0, 0)),
        ],
        out_specs=[
            pl.BlockSpec((None, tcs, vw), lambda b, g, t: (b, t, g)),
            pl.BlockSpec((None, 2 * hk, Dh, Dh), lambda b, g, t: (b, g, 0, 0)),
        ],
        out_shape=[jax.ShapeDtypeStruct((B, T, nh * Dh), BF16),
                   jax.ShapeDtypeStruct((B, nh, Dh, Dh), F32)],
        scratch_shapes=[pltpu.VMEM((2 * hk, Dh, Dh), F32)],
        compiler_params=_cparams(("parallel", "parallel", "arbitrary")),
        name="gdn_chunk_block",
    )(qkv, qkv, qkv, proj3, gates_t, gates_t, s0, norm_g.reshape(1, Dh))


GDN_CHUNK = 64


class _Cfg:
    def __init__(self, prompt):
        self.prompt = prompt
        if prompt:
            self.tm, self.tm_in, self.tf, self.conv_tt, self.conv_rc = 512, 1024, 512, 256, 64
            self.gdn_tt, self.gdn_rc, self.gdn_nc, self.gdn_hk = 256, 64, 4, 4
        else:
            self.tm, self.tm_in, self.tf, self.conv_tt, self.conv_rc = None, None, 1408, None, 8
            self.gdn_tt, self.gdn_rc, self.gdn_nc, self.gdn_hk = None, 8, 1, 4


def _conv_sb_mixer(proj3, i, cfg, conv_state, cache_kv, table, P):
    B, T, _ = proj3.shape
    ch = P['conv_a_w'].shape[-1]
    heads, Dh = cache_kv.shape[-2], cache_kv.shape[-1]
    sbw = heads * Dh
    qb, kb, vb = 2 * ch // Dh, (2 * ch + sbw) // Dh, (2 * ch + 2 * sbw) // Dh
    a16, cs = conv_a_block(proj3, conv_state, P['conv_a_w'][i], P['conv_a_b'][i], P['ln_a_g'][i],
                           P['ln_a_b'][i], cfg.conv_tt or T, cfg.conv_rc)
    kv = proj3[:, :, 2 * ch + sbw:].reshape(B, T, 2, heads, Dh)
    if cfg.prompt:
        o16 = sb_attention_prompt(proj3, P['sb_bias'][i], 512, 256, heads, qb, kb, vb)
    else:
        rows = T * heads
        q = proj3[:, :, 2 * ch:2 * ch + sbw].reshape(B, rows, Dh)
        pad = lambda x: jnp.pad(x.reshape(B, rows, Dh), ((0, 0), (0, LANES - rows), (0, 0)))
        o = sb_attention_sample(q, pad(kv[:, :, 0]), pad(kv[:, :, 1]), cache_kv, i, table, P['sb_bias'][i], 8)
        o16 = o.reshape(B, T, sbw).astype(BF16)
    return [a16.reshape(B * T, ch), o16.reshape(B * T, sbw)], cs, kv


def _gdn_mixer(proj3, ba, i, cfg, conv_state, s0, P):
    B, T, _ = proj3.shape
    nh = s0.shape[1]
    kw = P['w_out_gdn'].shape[1] // 2
    qkv_w = P['conv_c_w'].shape[-1]
    qkv, cs = gdn_prep_block(proj3, conv_state, P['conv_c_w'][i], kw, cfg.gdn_tt or T, cfg.gdn_rc, 1024)
    if T % GDN_CHUNK:
        tp = -(-T // GDN_CHUNK) * GDN_CHUNK
        padt = lambda x: jnp.pad(x, ((0, 0), (0, tp - T), (0, 0)))
        qkv, zsrc, z_col0, ba = padt(qkv), padt(proj3[:, :, qkv_w:]), 0, padt(ba)
    else:
        tp, zsrc, z_col0 = T, proj3, qkv_w
    gates = gdn_gates(ba.reshape(B * tp, 2 * nh), P['a_log'][i], P['dt_bias'][i], GDN_CHUNK, tp, T,
                      min(512, B * tp))
    gates_t = gates.reshape(B, tp, LANES)[:, :, :2 * nh].transpose(0, 2, 1).reshape(B, 2 * nh, 1, tp)
    og, s = gdn_chunk_block(qkv, zsrc, gates_t, s0, P['gdn_norm_g'][i], kw, z_col0, GDN_CHUNK, cfg.gdn_nc,
                            cfg.gdn_hk)
    return [og[:, :T].reshape(B * T, -1)], cs, s


def _trunk(x, mod, B, T, cfg, cache_kv, table, conv_a_state, conv_c_state, gdn_state, P):
    D = x.shape[-1]
    depth = P['norm_pre'].shape[0]
    tm = cfg.tm or B * T
    tm_in = cfg.tm_in or B * T
    tf = cfg.tf
    kv_rows, conv_a_new, conv_c_new, gdn_new = [], [], [], []
    for l in range(depth):
        i = l // 2
        x = ffn_block(x, mod, P['norm_pre'], P['norm_post'], P['wg'], P['wu'], P['wd'], l, 0, 0, tm, tf)
        if l % 2 == 0:
            proj = inproj_block(x, mod, P['norm_pre'], P['w_in_ab'], l, 1, tm_in, 1024)
            cat, cs, kv = _conv_sb_mixer(proj.reshape(B, T, -1), i, cfg, conv_a_state[i], cache_kv, table, P)
            conv_a_new.append(cs)
            kv_rows.append(kv)
            x = outproj_block(x, cat, mod, P['norm_post'], P['w_out_ab'], l, 1, tm, 1024)
        else:
            proj, ba = inproj_block(x, mod, P['norm_pre'], P['w_in_gdn'], l, 1, tm_in, 1024,
                                    w_extra=P['w_in_gdn_ba'])
            og, cs, s = _gdn_mixer(proj.reshape(B, T, -1), ba.reshape(B, T, -1), i, cfg, conv_c_state[i],
                                   gdn_state[i], P)
            conv_c_new.append(cs)
            gdn_new.append(s)
            x = outproj_block(x, og, mod, P['norm_post'], P['w_out_gdn'], l, 1, tm, 1024)
        x = ffn_block(x, mod, P['norm_pre'], P['norm_post'], P['wg'], P['wu'], P['wd'], l, 2, 1, tm, tf)
    return (x.reshape(B, T, D), jnp.stack(kv_rows), jnp.stack(conv_a_new), jnp.stack(conv_c_new),
            jnp.stack(gdn_new))


def kernel(x_prompt, x_sample, cache_kv_sb, state_conv_a, state_conv_c, state_gdn, page_table,
           c_prompt, c_sample, w_ada, b_ada, norm_pre, norm_post, w_ffn_gate, w_ffn_up, w_ffn_down,
           w_in_ab, conv_a_w, conv_a_b, ln_a_g, ln_a_b, w_out_ab, sb_bias,
           w_in_gdn, conv_c_w, a_log, dt_bias, gdn_norm_g, w_out_gdn):
    BP, T, D = x_prompt.shape
    BS, TS, _ = x_sample.shape
    L = w_ada.shape[0]
    n_sb, n_gdn = w_in_ab.shape[0], w_in_gdn.shape[0]
    dt = x_prompt.dtype

    R = -(-(BP + BS) // 8) * 8
    c_all = jnp.concatenate([c_prompt, c_sample, jnp.zeros((R - BP - BS, D), dt)], axis=0)
    mod_all = ada_modulation(c_all, w_ada, b_ada)
    mod_p = _Mod(mod_all[:, :, :BP].reshape(L, 9, BP, 1, D), T, False)
    mod_s = _Mod(jnp.repeat(mod_all[:, :, BP:BP + BS], TS, axis=2), TS, True)

    qkvz = conv_c_w.shape[-1] + w_out_gdn.shape[1]
    P = dict(
        norm_pre=norm_pre.reshape(L, 3, 1, D), norm_post=norm_post.reshape(L, 3, 1, D),
        wg=w_ffn_gate.astype(BF16), wu=w_ffn_up.astype(BF16), wd=w_ffn_down.astype(BF16),
        w_in_ab=w_in_ab.astype(BF16), w_out_ab=w_out_ab.astype(BF16),
        w_in_gdn=w_in_gdn.astype(BF16), w_in_gdn_ba=w_in_gdn[:, :, qkvz:].astype(BF16),
        w_out_gdn=w_out_gdn.astype(BF16),
        conv_a_w=conv_a_w, conv_a_b=conv_a_b, ln_a_g=ln_a_g, ln_a_b=ln_a_b, sb_bias=sb_bias,
        conv_c_w=conv_c_w, a_log=a_log, dt_bias=dt_bias, gdn_norm_g=gdn_norm_g,
    )

    y_p, kv_p, ca_p, cc_p, g_p = _trunk(
        x_prompt.reshape(BP * T, D), mod_p, BP, T, _Cfg(True), cache_kv_sb, None,
        jnp.zeros((n_sb, BP) + state_conv_a.shape[2:], dt), jnp.zeros((n_gdn, BP) + state_conv_c.shape[2:], dt),
        jnp.zeros((n_gdn, BP) + state_gdn.shape[2:], dt), P)
    y_s, kv_s, ca_s, cc_s, g_s = _trunk(
        x_sample.reshape(BS * TS, D), mod_s, BS, TS, _Cfg(False), cache_kv_sb, page_table,
        state_conv_a, state_conv_c, state_gdn, P)
    return (y_p, y_s, kv_p, ca_p, cc_p, g_p, kv_s, ca_s, cc_s, g_s)
```

```python
import functools
import math

import jax
import jax.numpy as jnp
from jax import lax
from jax.experimental import pallas as pl
from jax.experimental.pallas import tpu as pltpu

F32 = jnp.float32
BF16 = jnp.bfloat16
EPS = 1e-6
FFN_RESIDUAL = 0.5
VMEM_LIMIT = 56 * 1024 * 1024


def _cparams(sem):
    return pltpu.CompilerParams(dimension_semantics=sem, vmem_limit_bytes=VMEM_LIMIT)


def _silu(x):
    return x * jax.nn.sigmoid(x)


def _rms(x):
    return x * lax.rsqrt(jnp.mean(x * x, axis=-1, keepdims=True) + EPS)


ROWS = 16
ROWS_UNROLL = 4


def _rows_loop(n_rows, body):
    def step(r, carry):
        body(pl.multiple_of(r * ROWS, ROWS))
        return carry
    n = n_rows // ROWS
    lax.fori_loop(0, n, step, 0, unroll=math.gcd(n, ROWS_UNROLL))


def _mod_rows(ref, r0):
    return ref[...] if ref.shape[0] == 1 else ref[pl.ds(r0, ROWS), :]


def _prenorm_rows(x_ref, sh_ref, sc_ref, gpre_ref, h_ref):
    def body(r0):
        y = _rms(x_ref[pl.ds(r0, ROWS), :]) * gpre_ref[...]
        h_ref[pl.ds(r0, ROWS), :] = (y * (1.0 + _mod_rows(sc_ref, r0)) + _mod_rows(sh_ref, r0)).astype(BF16)
    _rows_loop(x_ref.shape[0], body)


def _postnorm_rows(acc_ref, x_ref, gt_ref, gpost_ref, o_ref, weight):
    def body(r0):
        rs = pl.ds(r0, ROWS)
        y = _rms(acc_ref[rs, :]) * gpost_ref[...]
        o_ref[rs, :] = x_ref[rs, :] + weight * (1.0 + _mod_rows(gt_ref, r0)) * y
    _rows_loop(x_ref.shape[0], body)


def _ada_kernel(c_ref, w_ref, b_ref, o_ref):
    s = _silu(c_ref[...]).astype(BF16)
    o_ref[...] = jnp.dot(s, w_ref[...].astype(BF16), preferred_element_type=F32) + b_ref[...]


def ada_modulation(c_all, w_ada, b_ada):
    L, D, N = w_ada.shape
    R = c_all.shape[0]
    nj = N // D
    return pl.pallas_call(
        _ada_kernel,
        grid=(L, nj),
        in_specs=[
            pl.BlockSpec((R, D), lambda l, j: (0, 0)),
            pl.BlockSpec((None, D, D), lambda l, j: (l, 0, j)),
            pl.BlockSpec((None, 1, D), lambda l, j: (l, 0, j)),
        ],
        out_specs=pl.BlockSpec((None, None, R, D), lambda l, j: (l, j, 0, 0)),
        out_shape=jax.ShapeDtypeStruct((L, nj, R, D), F32),
        compiler_params=_cparams(("parallel", "parallel")),
        name="ada_modulation",
    )(c_all, w_ada, b_ada.reshape(L, 1, N))


class _Mod:
    def __init__(self, arr, rows_per_seq, per_row):
        self.arr = arr
        self.rows_per_seq = rows_per_seq
        self.per_row = per_row

    def spec(self, l, k, tm, ngrid):
        D = self.arr.shape[-1]
        if self.per_row:
            def imap(i, *_):
                return (l, k, i, 0)
            return pl.BlockSpec((None, None, tm, D), imap)
        rps = self.rows_per_seq

        def imap(i, *_):
            return (l, k, (i * tm) // rps, 0, 0)
        return pl.BlockSpec((None, None, None, 1, D), imap)


def _vec_spec(l, s, D):
    return pl.BlockSpec((None, None, 1, D), lambda i, *_: (l, s, 0, 0))


def _ffn_kernel(x_ref, sh_ref, sc_ref, gt_ref, gpre_ref, gpost_ref, wg_ref, wu_ref, wd_ref,
                o_ref, h_ref, acc_ref, *, weight):
    f = pl.program_id(1)

    @pl.when(f == 0)
    def _():
        y = _rms(x_ref[...]) * gpre_ref[...]
        h_ref[...] = (y * (1.0 + sc_ref[...]) + sh_ref[...]).astype(BF16)
        acc_ref[...] = jnp.zeros_like(acc_ref)

    h = h_ref[...]
    g = jnp.dot(h, wg_ref[...], preferred_element_type=F32)
    u = jnp.dot(h, wu_ref[...], preferred_element_type=F32)
    a = (_silu(g) * u).astype(BF16)
    acc_ref[...] += jnp.dot(a, wd_ref[...], preferred_element_type=F32)

    @pl.when(f == pl.num_programs(1) - 1)
    def _():
        y = _rms(acc_ref[...]) * gpost_ref[...]
        o_ref[...] = x_ref[...] + weight * (1.0 + gt_ref[...]) * y


def ffn_block(x, mod, norm_pre, norm_post, wg, wu, wd, l, s, half, tm, tf):
    M, D = x.shape
    F = wg.shape[-1]
    grid = (M // tm, F // tf)
    k0 = 3 * s
    return pl.pallas_call(
        functools.partial(_ffn_kernel, weight=FFN_RESIDUAL),
        grid=grid,
        in_specs=[
            pl.BlockSpec((tm, D), lambda i, f: (i, 0)),
            mod.spec(l, k0 + 0, tm, grid), mod.spec(l, k0 + 1, tm, grid), mod.spec(l, k0 + 2, tm, grid),
            _vec_spec(l, s, D), _vec_spec(l, s, D),
            pl.BlockSpec((None, None, D, tf), lambda i, f: (l, half, 0, f)),
            pl.BlockSpec((None, None, D, tf), lambda i, f: (l, half, 0, f)),
            pl.BlockSpec((None, None, tf, D), lambda i, f: (l, half, f, 0)),
        ],
        out_specs=pl.BlockSpec((tm, D), lambda i, f: (i, 0)),
        out_shape=jax.ShapeDtypeStruct((M, D), F32),
        scratch_shapes=[pltpu.VMEM((tm, D), BF16), pltpu.VMEM((tm, D), F32)],
        compiler_params=_cparams(("parallel", "arbitrary")),
        name="ffn_block",
    )(x, mod.arr, mod.arr, mod.arr, norm_pre, norm_post, wg, wu, wd)


def _inproj_kernel(x_ref, sh_ref, sc_ref, gpre_ref, w_ref, *rest, has_extra):
    if has_extra:
        we_ref, o_ref, oe_ref, h_ref = rest
    else:
        o_ref, h_ref = rest
    j = pl.program_id(1)

    @pl.when(j == 0)
    def _():
        _prenorm_rows(x_ref, sh_ref, sc_ref, gpre_ref, h_ref)
        if has_extra:
            oe_ref[...] = jnp.dot(h_ref[...], we_ref[...], preferred_element_type=F32)

    o_ref[...] = jnp.dot(h_ref[...], w_ref[...], preferred_element_type=F32)


def inproj_block(x, mod, norm_pre, w, l, s, tm, tn, w_extra=None):
    M, D = x.shape
    N = w.shape[-1] // tn * tn
    i_kind = l // 2
    grid = (M // tm, N // tn)
    k0 = 3 * s
    in_specs = [
        pl.BlockSpec((tm, D), lambda i, j: (i, 0)),
        mod.spec(l, k0 + 0, tm, grid), mod.spec(l, k0 + 1, tm, grid),
        _vec_spec(l, s, D),
        pl.BlockSpec((None, D, tn), lambda i, j: (i_kind, 0, j)),
    ]
    args = [x, mod.arr, mod.arr, norm_pre, w]
    out_specs = [pl.BlockSpec((tm, tn), lambda i, j: (i, j))]
    out_shape = [jax.ShapeDtypeStruct((M, N), F32)]
    if w_extra is not None:
        ne = w_extra.shape[-1]
        in_specs.append(pl.BlockSpec((None, D, ne), lambda i, j: (i_kind, 0, 0)))
        args.append(w_extra)
        out_specs.append(pl.BlockSpec((tm, ne), lambda i, j: (i, 0)))
        out_shape.append(jax.ShapeDtypeStruct((M, ne), F32))
    res = pl.pallas_call(
        functools.partial(_inproj_kernel, has_extra=w_extra is not None),
        grid=grid,
        in_specs=in_specs,
        out_specs=out_specs,
        out_shape=out_shape,
        scratch_shapes=[pltpu.VMEM((tm, D), BF16)],
        compiler_params=_cparams(("parallel", "arbitrary")),
        name="inproj_block",
    )(*args)
    return res if w_extra is not None else res[0]


def _outproj_kernel(x_ref, gt_ref, gpost_ref, w_ref, *rest, starts):
    a_refs, (o_ref, acc_ref) = rest[:-2], rest[-2:]
    k = pl.program_id(1)

    @pl.when(k == 0)
    def _():
        acc_ref[...] = jnp.zeros_like(acc_ref)

    for p, a_ref in enumerate(a_refs):
        @pl.when((k >= starts[p]) & (k < starts[p + 1]))
        def _():
            acc_ref[...] += jnp.dot(a_ref[...], w_ref[...], preferred_element_type=F32)

    @pl.when(k == pl.num_programs(1) - 1)
    def _():
        _postnorm_rows(acc_ref, x_ref, gt_ref, gpost_ref, o_ref, 1.0)


def outproj_block(x, a_parts, mod, norm_post, w, l, s, tm, tk):
    M, D = x.shape
    i_kind = l // 2
    starts = [0]
    for a in a_parts:
        starts.append(starts[-1] + a.shape[-1] // tk)
    grid = (M // tm, starts[-1])

    def a_spec(p):
        lo, n = starts[p], starts[p + 1] - starts[p]
        return pl.BlockSpec((tm, tk), lambda i, k: (i, jnp.clip(k - lo, 0, n - 1)))

    return pl.pallas_call(
        functools.partial(_outproj_kernel, starts=tuple(starts)),
        grid=grid,
        in_specs=[
            pl.BlockSpec((tm, D), lambda i, k: (i, 0)),
            mod.spec(l, 3 * s + 2, tm, grid),
            _vec_spec(l, s, D),
            pl.BlockSpec((None, tk, D), lambda i, k: (i_kind, k, 0)),
        ] + [a_spec(p) for p in range(len(a_parts))],
        out_specs=pl.BlockSpec((tm, D), lambda i, k: (i, 0)),
        out_shape=jax.ShapeDtypeStruct((M, D), F32),
        scratch_shapes=[pltpu.VMEM((tm, D), F32)],
        compiler_params=_cparams(("parallel", "arbitrary")),
        name="outproj_block",
    )(x, mod.arr, norm_post, w, *a_parts)


CONV_HIST = 32
LANES = 128


def _shifted_taps(ext_ref, w_ref, r0, cs, rc, hist_pad, off, taps, acc):
    n = rc + hist_pad
    win = ext_ref[pl.ds(r0, n), cs]
    for phase in range(8):
        wb = None
        for a8 in range(hist_pad // 8 + 1):
            j = 8 * a8 + phase - off
            if 0 <= j < taps:
                if wb is None:
                    wb = win if phase == 0 else pltpu.roll(win, n - phase, 0)
                acc = acc + w_ref[j:j + 1, cs] * wb[8 * a8:8 * a8 + rc]
    return acc


def _conv_a_kernel(val_ref, gate_ref, st_ref, w_ref, b_ref, lg_ref, lb_ref, a_ref, ns_ref,
                   ext_ref, y_ref, *, tt, rc, taps):
    t = pl.program_id(1)
    hist = taps - 1
    off = CONV_HIST - hist
    ch = val_ref.shape[-1]

    @pl.when(t == 0)
    def _():
        ext_ref[off:CONV_HIST, :] = st_ref[...]

    @pl.when(t > 0)
    def _():
        ext_ref[0:CONV_HIST, :] = ext_ref[tt:tt + CONV_HIST, :]

    ext_ref[CONV_HIST:CONV_HIST + tt, :] = val_ref[...] * jax.nn.sigmoid(gate_ref[...])
    if tt % rc:
        ext_ref[CONV_HIST + tt:, :] = jnp.zeros((ext_ref.shape[0] - CONV_HIST - tt, ch), F32)

    def chunk(r, carry):
        r0 = pl.multiple_of(r * rc, rc)
        for c in range(ch // LANES):
            cs = slice(c * LANES, (c + 1) * LANES)
            acc = _shifted_taps(ext_ref, w_ref, r0, cs, rc, CONV_HIST, off, taps,
                                jnp.broadcast_to(b_ref[:, cs], (rc, LANES)))
            rows = min(rc, tt)
            y_ref[pl.ds(r0, rows), cs] = acc[:rows]
        return carry

    lax.fori_loop(0, pl.cdiv(tt, rc), chunk, 0)

    y = y_ref[...]
    mu = jnp.mean(y, axis=-1, keepdims=True)
    d = y - mu
    var = jnp.mean(d * d, axis=-1, keepdims=True)
    a_ref[...] = _silu(d * lax.rsqrt(var + EPS) * lg_ref[...] + lb_ref[...]).astype(a_ref.dtype)

    @pl.when(t == pl.num_programs(1) - 1)
    def _():
        ns_ref[...] = ext_ref[tt + off:tt + CONV_HIST, :]


def conv_a_block(proj3, conv_state, conv_w, conv_b, ln_g, ln_b, tt, rc):
    B, T, _ = proj3.shape
    taps, CH = conv_w.shape
    row = lambda v: v.reshape(1, CH)
    return pl.pallas_call(
        functools.partial(_conv_a_kernel, tt=tt, rc=rc, taps=taps),
        grid=(B, T // tt),
        in_specs=[
            pl.BlockSpec((None, tt, CH), lambda b, t: (b, t, 0)),
            pl.BlockSpec((None, tt, CH), lambda b, t: (b, t, 1)),
            pl.BlockSpec((None, taps - 1, CH), lambda b, t: (b, 0, 0)),
            pl.BlockSpec((taps, CH), lambda b, t: (0, 0)),
            pl.BlockSpec((1, CH), lambda b, t: (0, 0)),
            pl.BlockSpec((1, CH), lambda b, t: (0, 0)),
            pl.BlockSpec((1, CH), lambda b, t: (0, 0)),
        ],
        out_specs=[
            pl.BlockSpec((None, tt, CH), lambda b, t: (b, t, 0)),
            pl.BlockSpec((None, taps - 1, CH), lambda b, t: (b, 0, 0)),
        ],
        out_shape=[jax.ShapeDtypeStruct((B, T, CH), BF16),
                   jax.ShapeDtypeStruct((B, taps - 1, CH), F32)],
        scratch_shapes=[pltpu.VMEM((CONV_HIST + pl.cdiv(tt, rc) * rc, CH), F32), pltpu.VMEM((tt, CH), F32)],
        compiler_params=_cparams(("parallel", "arbitrary")),
        name="conv_a_block",
    )(proj3, proj3, conv_state, conv_w, row(conv_b), row(ln_g), row(ln_b))


def _log_sigmoid_pair(z):
    l = jnp.log(1.0 + jnp.exp(-jnp.abs(z)))
    return jnp.minimum(z, 0.0) - l, jnp.minimum(-z, 0.0) - l


def _suffix_sum(lk, u2_ref):
    hi = lk.astype(BF16)
    lo = (lk - hi.astype(F32)).astype(BF16)
    return jnp.dot(jnp.concatenate([hi, lo], axis=1), u2_ref[...], preferred_element_type=F32)


def _sb_prompt_kernel(bias_ref, q_ref, k_ref, v_ref, u_ref, o_ref, k16_ref, v16_ref, *, tq, tk, scale):
    h = pl.program_id(1)
    i = pl.program_id(2)

    @pl.when(i == 0)
    def _():
        k16_ref[...] = k_ref[...].astype(BF16)
        v16_ref[...] = v_ref[...].astype(BF16)

    bias = bias_ref[h]
    q16 = q_ref[...].astype(BF16)
    nd = tq // tk

    def sweep(j_hi, m, acc, diagonal):
        starts = [j_hi - (n + 1) * tk for n in range(nd)]
        first = [(nd - 1 - n) * tk if diagonal else 0 for n in range(nd)]
        lss, lks = [], []
        for n, j0 in enumerate(starts):
            kb = k16_ref[pl.ds(j0, tk), :]
            z = lax.dot_general(q16[first[n]:], kb, (((1,), (1,)), ((), ())),
                                preferred_element_type=F32) * scale + bias
            ls, lk = _log_sigmoid_pair(z)
            if diagonal:
                qpos = lax.broadcasted_iota(jnp.int32, z.shape, 0)
                kpos = lax.broadcasted_iota(jnp.int32, z.shape, 1)
                vis = kpos < qpos
                ls = jnp.where(vis, ls, -jnp.inf)
                lk = jnp.where(vis, lk, 0.0)
            lss.append(ls)
            lks.append(lk)
        laters = [_suffix_sum(lk, u_ref) for lk in lks]
        for n, j0 in enumerate(starts):
            r0 = first[n]
            w = jnp.exp(lss[n] + laters[n] + m[r0:])
            pv = jnp.dot(w.astype(BF16), v16_ref[pl.ds(j0, tk), :], preferred_element_type=F32)
            dm = laters[n][:, 0:1] + lks[n][:, 0:1]
            if r0:
                acc = jnp.concatenate([acc[:r0], acc[r0:] + pv], axis=0)
                m = jnp.concatenate([m[:r0], m[r0:] + dm], axis=0)
            else:
                acc, m = acc + pv, m + dm
        return m, acc

    base = pl.multiple_of(i * tq, tq)
    m, acc = sweep(base + tq, jnp.zeros((tq, 1), F32), jnp.zeros((tq, o_ref.shape[-1]), F32), True)

    def body(jj, carry):
        return sweep(pl.multiple_of(base - jj * tq, tq), carry[0], carry[1], False)

    m, acc = lax.fori_loop(0, i, body, (m, acc))
    o_ref[...] = acc.astype(o_ref.dtype)


def _suffix_matrix(n):
    r = lax.broadcasted_iota(jnp.int32, (2 * n, n), 0) % n
    c = lax.broadcasted_iota(jnp.int32, (2 * n, n), 1)
    return (r > c).astype(BF16)


def sb_attention_prompt(proj3, sb_bias, tq, tk, heads, q_col, k_col, v_col):
    B, T, _ = proj3.shape
    Dh = LANES
    return pl.pallas_call(
        functools.partial(_sb_prompt_kernel, tq=tq, tk=tk, scale=Dh ** -0.5),
        grid=(B, heads, T // tq),
        in_specs=[
            pl.BlockSpec(memory_space=pltpu.SMEM),
            pl.BlockSpec((None, tq, Dh), lambda b, h, i: (b, i, q_col + h)),
            pl.BlockSpec((None, T, Dh), lambda b, h, i: (b, 0, k_col + h)),
            pl.BlockSpec((None, T, Dh), lambda b, h, i: (b, 0, v_col + h)),
            pl.BlockSpec((2 * tk, tk), lambda b, h, i: (0, 0)),
        ],
        out_specs=pl.BlockSpec((None, tq, Dh), lambda b, h, i: (b, i, h)),
        out_shape=jax.ShapeDtypeStruct((B, T, heads * Dh), BF16),
        scratch_shapes=[pltpu.VMEM((T, Dh), BF16), pltpu.VMEM((T, Dh), BF16)],
        compiler_params=_cparams(("parallel", "parallel", "arbitrary")),
        name="sb_attention_prompt",
    )(sb_bias, proj3, proj3, proj3, _suffix_matrix(tk))


def _sb_sample_kernel(tbl_ref, q_ref, knew_ref, vnew_ref, bias_ref, u_ref, *rest, pp, page, heads, scale):
    page_refs = rest[:pp]
    o_ref, m_ref = rest[pp], rest[pp + 1]
    g = pl.program_id(1)
    R = q_ref.shape[0]
    q16 = q_ref[...].astype(BF16)
    bias = bias_ref[...]
    rhead = lax.broadcasted_iota(jnp.int32, (R, LANES), 0) % heads
    rtime = lax.broadcasted_iota(jnp.int32, (R, LANES), 0) // heads
    ccol = lax.broadcasted_iota(jnp.int32, (R, LANES), 1)
    same_head = (ccol % heads) == rhead

    def sweep(blocks, m, acc, vis):
        nch = blocks[0][0].shape[0] // LANES
        order = [(b, c) for b in range(len(blocks)) for c in reversed(range(nch))]
        lss, lks = {}, {}
        for b, (k2d, _) in enumerate(blocks):
            z = lax.dot_general(q16, k2d.astype(BF16), (((1,), (1,)), ((), ())),
                                preferred_element_type=F32) * scale + bias
            ls, lk = _log_sigmoid_pair(z)
            for c in range(nch):
                lss[b, c] = jnp.where(vis, ls[:, c * LANES:(c + 1) * LANES], -jnp.inf)
                lks[b, c] = jnp.where(vis, lk[:, c * LANES:(c + 1) * LANES], 0.0)
        later = _suffix_sum(jnp.concatenate([lks[bc] for bc in order], axis=0), u_ref)
        ws = {}
        for n_, bc in enumerate(order):
            lat = later[n_ * R:(n_ + 1) * R]
            ws[bc] = jnp.exp(lss[bc] + lat + m).astype(BF16)
            m = m + lat[:, 0:1] + lks[bc][:, 0:1]
        for b, (_, v2d) in enumerate(blocks):
            w = jnp.concatenate([ws[b, c] for c in range(nch)], axis=1)
            acc = acc + jnp.dot(w, v2d.astype(BF16), preferred_element_type=F32)
        return m, acc

    @pl.when(g == 0)
    def _():
        vis_new = same_head & ((ccol // heads) < rtime)
        m, acc = sweep([(knew_ref[...], vnew_ref[...])], jnp.zeros((R, 1), F32),
                       jnp.zeros(o_ref.shape, F32), vis_new)
        m_ref[...] = m
        o_ref[...] = acc

    blocks = [(pr[:, 0].reshape(page * heads, LANES), pr[:, 1].reshape(page * heads, LANES))
              for pr in page_refs]
    m, acc = sweep(blocks, m_ref[...], o_ref[...], same_head)
    m_ref[...] = m
    o_ref[...] = acc


def sb_attention_sample(q, k_new, v_new, cache_kv, layer_idx, page_table, sb_bias, pp):
    B, R, Dh = q.shape
    _, _, page, _, heads, _ = cache_kv.shape
    n_pages = page_table.shape[1]
    bias_col = jnp.tile(sb_bias, R // heads).reshape(R, 1)

    def page_spec(u):
        def imap(b, g, tbl):
            return (layer_idx, tbl[b, n_pages - 1 - (g * pp + u)], 0, 0, 0, 0)
        return pl.BlockSpec((None, None, page, 2, heads, Dh), imap)

    grid_spec = pltpu.PrefetchScalarGridSpec(
        num_scalar_prefetch=1,
        grid=(B, n_pages // pp),
        in_specs=[
            pl.BlockSpec((None, R, Dh), lambda b, g, tbl: (b, 0, 0)),
            pl.BlockSpec((None, LANES, Dh), lambda b, g, tbl: (b, 0, 0)),
            pl.BlockSpec((None, LANES, Dh), lambda b, g, tbl: (b, 0, 0)),
            pl.BlockSpec((R, 1), lambda b, g, tbl: (0, 0)),
            pl.BlockSpec((2 * LANES, LANES), lambda b, g, tbl: (0, 0)),
        ] + [page_spec(u) for u in range(pp)],
        out_specs=pl.BlockSpec((None, R, Dh), lambda b, g, tbl: (b, 0, 0)),
        scratch_shapes=[pltpu.VMEM((R, 1), F32)],
    )
    return pl.pallas_call(
        functools.partial(_sb_sample_kernel, pp=pp, page=page, heads=heads, scale=Dh ** -0.5),
        grid_spec=grid_spec,
        out_shape=jax.ShapeDtypeStruct((B, R, Dh), F32),
        compiler_params=_cparams(("parallel", "arbitrary")),
        name="sb_attention_sample",
    )(page_table, q, k_new, v_new, bias_col, _suffix_matrix(LANES), *([cache_kv] * pp))


GDN_HIST = 8


def _gdn_prep_kernel(x_ref, st_ref, w_ref, o_ref, ns_ref, ext_ref, *, tt, rc, taps, n_q, n_k, scale):
    j = pl.program_id(1)
    t = pl.program_id(2)
    hist = taps - 1
    off = GDN_HIST - hist
    tc = x_ref.shape[-1]

    @pl.when(t == 0)
    def _():
        ext_ref[off:GDN_HIST, :] = st_ref[...]

    @pl.when(t > 0)
    def _():
        ext_ref[0:GDN_HIST, :] = ext_ref[tt:tt + GDN_HIST, :]

    ext_ref[GDN_HIST:GDN_HIST + tt, :] = x_ref[...]
    if tt % rc:
        ext_ref[GDN_HIST + tt:, :] = jnp.zeros((ext_ref.shape[0] - GDN_HIST - tt, tc), F32)

    rows = min(rc, tt)

    def run(normalise, post_scale):
        def chunk(r, carry):
            r0 = pl.multiple_of(r * rc, rc)
            for c in range(tc // LANES):
                cs = slice(c * LANES, (c + 1) * LANES)
                y = _silu(_shifted_taps(ext_ref, w_ref, r0, cs, rc, GDN_HIST, off, taps,
                                        jnp.zeros((rc, LANES), F32)))
                if normalise:
                    y = y * lax.rsqrt(jnp.sum(y * y, axis=-1, keepdims=True) + EPS)
                    if post_scale != 1.0:
                        y = y * post_scale
                o_ref[pl.ds(r0, rows), cs] = y[:rows]
            return carry
        lax.fori_loop(0, pl.cdiv(tt, rc), chunk, 0)

    @pl.when(j < n_q)
    def _():
        run(True, scale)

    @pl.when((j >= n_q) & (j < n_q + n_k))
    def _():
        run(True, 1.0)

    @pl.when(j >= n_q + n_k)
    def _():
        run(False, 1.0)

    @pl.when(t == pl.num_programs(2) - 1)
    def _():
        ns_ref[...] = ext_ref[tt + off:tt + GDN_HIST, :]


def gdn_prep_block(proj3, conv_state, conv_w, kw, tt, rc, tc):
    B, T, _ = proj3.shape
    taps, QKV = conv_w.shape
    return pl.pallas_call(
        functools.partial(_gdn_prep_kernel, tt=tt, rc=rc, taps=taps, n_q=kw // tc, n_k=kw // tc,
                          scale=LANES ** -0.5),
        grid=(B, QKV // tc, T // tt),
        in_specs=[
            pl.BlockSpec((None, tt, tc), lambda b, j, t: (b, t, j)),
            pl.BlockSpec((None, taps - 1, tc), lambda b, j, t: (b, 0, j)),
            pl.BlockSpec((taps, tc), lambda b, j, t: (0, j)),
        ],
        out_specs=[
            pl.BlockSpec((None, tt, tc), lambda b, j, t: (b, t, j)),
            pl.BlockSpec((None, taps - 1, tc), lambda b, j, t: (b, 0, j)),
        ],
        out_shape=[jax.ShapeDtypeStruct((B, T, QKV), F32),
                   jax.ShapeDtypeStruct((B, taps - 1, QKV), F32)],
        scratch_shapes=[pltpu.VMEM((GDN_HIST + pl.cdiv(tt, rc) * rc, tc), F32)],
        compiler_params=_cparams(("parallel", "parallel", "arbitrary")),
        name="gdn_prep_block",
    )(proj3, conv_state, conv_w)


def _split3(x):
    hi = x.astype(BF16)
    r = x - hi.astype(F32)
    mid = r.astype(BF16)
    lo = (r - mid.astype(F32)).astype(BF16)
    return hi, mid, lo


def _gdn_gates_kernel(ba_ref, alog_ref, dtb_ref, tri_ref, o_ref, *, nh, seq_rows, valid_rows):
    ba = ba_ref[...]
    R = ba.shape[0]
    beta = jax.nn.sigmoid(ba[:, :nh])
    x = ba[:, nh:] + dtb_ref[...]
    softplus = jnp.maximum(x, 0.0) + jnp.log1p(jnp.exp(-jnp.abs(x)))
    g = -jnp.exp(alog_ref[...]) * softplus
    if valid_rows < seq_rows:
        row = lax.broadcasted_iota(jnp.int32, (R, nh), 0) % seq_rows
        beta = jnp.where(row < valid_rows, beta, 0.0)
        g = jnp.where(row < valid_rows, g, 0.0)
    tri = tri_ref[...]
    gc = sum(jnp.dot(tri, p, preferred_element_type=F32) for p in _split3(g))
    o_ref[...] = jnp.concatenate([gc, beta, jnp.zeros((R, LANES - 2 * nh), F32)], axis=1)


def gdn_gates(ba, a_log, dt_bias, chunk, seq_rows, valid_rows, tr):
    M, two_nh = ba.shape
    nh = two_nh // 2
    r = lax.broadcasted_iota(jnp.int32, (tr, tr), 0)
    c = lax.broadcasted_iota(jnp.int32, (tr, tr), 1)
    tri = ((r // chunk == c // chunk) & (r >= c)).astype(BF16)
    return pl.pallas_call(
        functools.partial(_gdn_gates_kernel, nh=nh, seq_rows=seq_rows, valid_rows=valid_rows),
        grid=(M // tr,),
        in_specs=[
            pl.BlockSpec((tr, two_nh), lambda i: (i, 0)),
            pl.BlockSpec((1, nh), lambda i: (0, 0)),
            pl.BlockSpec((1, nh), lambda i: (0, 0)),
            pl.BlockSpec((tr, tr), lambda i: (0, 0)),
        ],
        out_specs=pl.BlockSpec((tr, LANES), lambda i: (i, 0)),
        out_shape=jax.ShapeDtypeStruct((M, LANES), F32),
        compiler_params=_cparams(("parallel",)),
        name="gdn_gates",
    )(ba, a_log.reshape(1, nh), dt_bias.reshape(1, nh), tri)


def _gdn_chunk_kernel(q_ref, k_ref, v_ref, z_ref, gc_ref, bt_ref, s0_ref, ng_ref, o_ref, so_ref, s_ref,
                      *, nc, hk, C):
    t = pl.program_id(2)
    Dh = LANES
    C2 = 2 * C

    @pl.when(t == 0)
    def _():
        s_ref[...] = s0_ref[...]

    ii = lax.broadcasted_iota(jnp.int32, (C, C2), 0)
    ll = lax.broadcasted_iota(jnp.int32, (C, C2), 1)
    jj = ll % C
    left = ll < C
    causal = ii >= jj
    strict = ii > jj
    diag = ii == jj
    diag_of = [ii == ll, ii + C == ll]
    half_of = [left, ll >= C]
    eye2 = diag.astype(F32)
    nt = (((1,), (1,)), ((), ()))
    tn = (((0,), (0,)), ((), ()))

    def block_diag(p2):
        return jnp.concatenate([jnp.where(left, p2, 0.0), jnp.where(left, 0.0, p2)], axis=0).astype(BF16)

    units = [(kk, c) for kk in range(hk) for c in range(nc)]
    kq, gcol, bcol, grow2, decay2, attn2, pw, xinv = {}, {}, {}, {}, {}, {}, {}, {}

    for un in units:
        kk, c = un
        rs = slice(c * C, (c + 1) * C)
        ks = slice(kk * Dh, (kk + 1) * Dh)
        kc, qc = k_ref[rs, ks], q_ref[rs, ks]
        kq[un] = (kc, qc)
        k16 = kc.astype(BF16)
        k16x2 = jnp.concatenate([k16, k16], axis=0)
        gkk2 = lax.dot_general(k16, k16x2, nt, preferred_element_type=F32)
        gqk2 = lax.dot_general(qc.astype(BF16), k16x2, nt, preferred_element_type=F32)
        g2 = jnp.concatenate([gc_ref[2 * kk, :, rs], gc_ref[2 * kk + 1, :, rs]], axis=1)
        b2 = jnp.concatenate([bt_ref[2 * kk, :, rs], bt_ref[2 * kk + 1, :, rs]], axis=1)
        gcol[un] = [jnp.sum(jnp.where(dm, g2, 0.0), axis=1, keepdims=True) for dm in diag_of]
        bcol[un] = [jnp.sum(jnp.where(dm, b2, 0.0), axis=1, keepdims=True) for dm in diag_of]
        gcol2 = jnp.where(left, gcol[un][0], gcol[un][1])
        bcol2 = jnp.where(left, bcol[un][0], bcol[un][1])
        grow2[un] = g2
        decay2[un] = jnp.exp(jnp.where(causal, gcol2 - g2, -jnp.inf))
        a2 = jnp.where(strict, gkk2 * bcol2 * decay2[un], 0.0)
        attn2[un] = gqk2 * decay2[un]
        pw[un] = a2
        xinv[un] = eye2 - a2

    k = 2
    while k < C:
        for un in units:
            pw[un] = jnp.dot(pw[un].astype(BF16), block_diag(pw[un]), preferred_element_type=F32)
        for un in units:
            xinv[un] = xinv[un] + jnp.dot(xinv[un].astype(BF16), block_diag(pw[un]),
                                          preferred_element_type=F32)
        k *= 2

    heads = [(kk, e) for kk in range(hk) for e in range(2)]
    uw, qd, kd, gl = {}, {}, {}, {}
    for un in units:
        kk, c = un
        rs = slice(c * C, (c + 1) * C)
        kc, qc = kq[un]
        rhs = []
        for e in range(2):
            vs = slice((2 * kk + e) * Dh, (2 * kk + e + 1) * Dh)
            bc, gcl = bcol[un][e], gcol[un][e]
            egc = jnp.exp(gcl)
            glast = grow2[un][:, e * C + C - 1:e * C + C]
            rhs.append(jnp.concatenate([v_ref[rs, vs] * bc, kc * (bc * egc)], axis=1).astype(BF16))
            qd[kk, e, c] = (qc * egc).astype(BF16)
            kd[kk, e, c] = (kc * jnp.exp(glast - gcl)).astype(BF16)
            gl[kk, e, c] = jnp.exp(glast)
        rhs2 = jnp.concatenate(rhs, axis=0)
        for e in range(2):
            uw[kk, e, c] = jnp.dot(jnp.where(half_of[e], xinv[un], 0.0).astype(BF16), rhs2,
                                   preferred_element_type=F32)
            attn2[kk, e, c] = jnp.where(half_of[e], attn2[un], 0.0).astype(BF16)

    S = {h: s_ref[2 * h[0] + h[1]] for h in heads}
    for c in range(nc):
        rs = slice(c * C, (c + 1) * C)
        s16 = {h: S[h].astype(BF16) for h in heads}
        v_new = {h: uw[h[0], h[1], c][:, :Dh]
                 - jnp.dot(uw[h[0], h[1], c][:, Dh:].astype(BF16), s16[h], preferred_element_type=F32)
                 for h in heads}
        for h in heads:
            kk, e = h
            S[h] = S[h] * gl[kk, e, c] + lax.dot_general(kd[kk, e, c], v_new[h].astype(BF16), tn,
                                                          preferred_element_type=F32)
        vn2 = {kk: jnp.concatenate([v_new[kk, 0], v_new[kk, 1]], axis=0).astype(BF16) for kk in range(hk)}
        for h in heads:
            kk, e = h
            vs = slice((2 * kk + e) * Dh, (2 * kk + e + 1) * Dh)
            o = (jnp.dot(qd[kk, e, c], s16[h], preferred_element_type=F32)
                 + jnp.dot(attn2[kk, e, c], vn2[kk], preferred_element_type=F32))
            o = _rms(o) * ng_ref[...] * _silu(z_ref[rs, vs])
            o_ref[rs, vs] = o.astype(o_ref.dtype)
    for h in heads:
        s_ref[2 * h[0] + h[1]] = S[h]

    @pl.when(t == pl.num_programs(2) - 1)
    def _():
        so_ref[...] = s_ref[...]


def gdn_chunk_block(qkv, proj3, gates_t, s0, norm_g, kw, z_col0, chunk, nc, hk):
    B, T, QKV = qkv.shape
    Dh = LANES
    nh = s0.shape[1]
    kh = kw // Dh
    tcs = nc * chunk
    qw, vw = hk * Dh, 2 * hk * Dh
    return pl.pallas_call(
        functools.partial(_gdn_chunk_kernel, nc=nc, hk=hk, C=chunk),
        grid=(B, kh // hk, T // tcs),
        in_specs=[
            pl.BlockSpec((None, tcs, qw), lambda b, g, t: (b, t, g)),
            pl.BlockSpec((None, tcs, qw), lambda b, g, t: (b, t, kw // qw + g)),
            pl.BlockSpec((None, tcs, vw), lambda b, g, t: (b, t, 2 * kw // vw + g)),
            pl.BlockSpec((None, tcs, vw), lambda b, g, t: (b, t, z_col0 // vw + g)),
            pl.BlockSpec((None, 2 * hk, 1, tcs), lambda b, g, t: (b, g, 0, t)),
            pl.BlockSpec((None, 2 * hk, 1, tcs), lambda b, g, t: (b, nh // (2 * hk) + g, 0, t)),
            pl.BlockSpec((None, 2 * hk, Dh, Dh), lambda b, g, t: (b, g, 0, 0)),
            pl.BlockSpec((1, Dh), lambda b, g, t: (0, 0)),
        ],
        out_specs=[
            pl.BlockSpec((None, tcs, vw), lambda b, g, t: (b, t, g)),
            pl.BlockSpec((None, 2 * hk, Dh, Dh), lambda b, g, t: (b, g, 0, 0)),
        ],
        out_shape=[jax.ShapeDtypeStruct((B, T, nh * Dh), BF16),
                   jax.ShapeDtypeStruct((B, nh, Dh, Dh), F32)],
        scratch_shapes=[pltpu.VMEM((2 * hk, Dh, Dh), F32)],
        compiler_params=_cparams(("parallel", "parallel", "arbitrary")),
        name="gdn_chunk_block",
    )(qkv, qkv, qkv, proj3, gates_t, gates_t, s0, norm_g.reshape(1, Dh))


GDN_CHUNK = 64


class _Cfg:
    def __init__(self, prompt):
        self.prompt = prompt
        if prompt:
            self.tm, self.tm_in, self.tf, self.conv_tt, self.conv_rc = 512, 1024, 512, 256, 64
            self.gdn_tt, self.gdn_rc, self.gdn_nc, self.gdn_hk = 256, 64, 4, 8
        else:
            self.tm, self.tm_in, self.tf, self.conv_tt, self.conv_rc = None, None, 1408, None, 8
            self.gdn_tt, self.gdn_rc, self.gdn_nc, self.gdn_hk = None, 8, 1, 4


def _conv_sb_mixer(proj3, i, cfg, conv_state, cache_kv, table, P):
    B, T, _ = proj3.shape
    ch = P['conv_a_w'].shape[-1]
    heads, Dh = cache_kv.shape[-2], cache_kv.shape[-1]
    sbw = heads * Dh
    qb, kb, vb = 2 * ch // Dh, (2 * ch + sbw) // Dh, (2 * ch + 2 * sbw) // Dh
    a16, cs = conv_a_block(proj3, conv_state, P['conv_a_w'][i], P['conv_a_b'][i], P['ln_a_g'][i],
                           P['ln_a_b'][i], cfg.conv_tt or T, cfg.conv_rc)
    kv = proj3[:, :, 2 * ch + sbw:].reshape(B, T, 2, heads, Dh)
    if cfg.prompt:
        o16 = sb_attention_prompt(proj3, P['sb_bias'][i], 512, 256, heads, qb, kb, vb)
    else:
        rows = T * heads
        q = proj3[:, :, 2 * ch:2 * ch + sbw].reshape(B, rows, Dh)
        pad = lambda x: jnp.pad(x.reshape(B, rows, Dh), ((0, 0), (0, LANES - rows), (0, 0)))
        o = sb_attention_sample(q, pad(kv[:, :, 0]), pad(kv[:, :, 1]), cache_kv, i, table, P['sb_bias'][i], 8)
        o16 = o.reshape(B, T, sbw).astype(BF16)
    return [a16.reshape(B * T, ch), o16.reshape(B * T, sbw)], cs, kv


def _gdn_mixer(proj3, ba, i, cfg, conv_state, s0, P):
    B, T, _ = proj3.shape
    nh = s0.shape[1]
    kw = P['w_out_gdn'].shape[1] // 2
    qkv_w = P['conv_c_w'].shape[-1]
    qkv, cs = gdn_prep_block(proj3, conv_state, P['conv_c_w'][i], kw, cfg.gdn_tt or T, cfg.gdn_rc, 1024)
    if T % GDN_CHUNK:
        tp = -(-T // GDN_CHUNK) * GDN_CHUNK
        padt = lambda x: jnp.pad(x, ((0, 0), (0, tp - T), (0, 0)))
        qkv, zsrc, z_col0, ba = padt(qkv), padt(proj3[:, :, qkv_w:]), 0, padt(ba)
    else:
        tp, zsrc, z_col0 = T, proj3, qkv_w
    gates = gdn_gates(ba.reshape(B * tp, 2 * nh), P['a_log'][i], P['dt_bias'][i], GDN_CHUNK, tp, T,
                      min(512, B * tp))
    gates_t = gates.reshape(B, tp, LANES)[:, :, :2 * nh].transpose(0, 2, 1).reshape(B, 2 * nh, 1, tp)
    og, s = gdn_chunk_block(qkv, zsrc, gates_t, s0, P['gdn_norm_g'][i], kw, z_col0, GDN_CHUNK, cfg.gdn_nc,
                            cfg.gdn_hk)
    return [og[:, :T].reshape(B * T, -1)], cs, s


def _trunk(x, mod, B, T, cfg, cache_kv, table, conv_a_state, conv_c_state, gdn_state, P):
    D = x.shape[-1]
    depth = P['norm_pre'].shape[0]
    tm = cfg.tm or B * T
    tm_in = cfg.tm_in or B * T
    tf = cfg.tf
    kv_rows, conv_a_new, conv_c_new, gdn_new = [], [], [], []
    for l in range(depth):
        i = l // 2
        x = ffn_block(x, mod, P['norm_pre'], P['norm_post'], P['wg'], P['wu'], P['wd'], l, 0, 0, tm, tf)
        if l % 2 == 0:
            proj = inproj_block(x, mod, P['norm_pre'], P['w_in_ab'], l, 1, tm_in, 1024)
            cat, cs, kv = _conv_sb_mixer(proj.reshape(B, T, -1), i, cfg, conv_a_state[i], cache_kv, table, P)
            conv_a_new.append(cs)
            kv_rows.append(kv)
            x = outproj_block(x, cat, mod, P['norm_post'], P['w_out_ab'], l, 1, tm, 1024)
        else:
            proj, ba = inproj_block(x, mod, P['norm_pre'], P['w_in_gdn'], l, 1, tm_in, 1024,
                                    w_extra=P['w_in_gdn_ba'])
            og, cs, s = _gdn_mixer(proj.reshape(B, T, -1), ba.reshape(B, T, -1), i, cfg, conv_c_state[i],
                                   gdn_state[i], P)
            conv_c_new.append(cs)
            gdn_new.append(s)
            x = outproj_block(x, og, mod, P['norm_post'], P['w_out_gdn'], l, 1, tm, 1024)
        x = ffn_block(x, mod, P['norm_pre'], P['norm_post'], P['wg'], P['wu'], P['wd'], l, 2, 1, tm, tf)
    return (x.reshape(B, T, D), jnp.stack(kv_rows), jnp.stack(conv_a_new), jnp.stack(conv_c_new),
            jnp.stack(gdn_new))


def kernel(x_prompt, x_sample, cache_kv_sb, state_conv_a, state_conv_c, state_gdn, page_table,
           c_prompt, c_sample, w_ada, b_ada, norm_pre, norm_post, w_ffn_gate, w_ffn_up, w_ffn_down,
           w_in_ab, conv_a_w, conv_a_b, ln_a_g, ln_a_b, w_out_ab, sb_bias,
           w_in_gdn, conv_c_w, a_log, dt_bias, gdn_norm_g, w_out_gdn):
    BP, T, D = x_prompt.shape
    BS, TS, _ = x_sample.shape
    L = w_ada.shape[0]
    n_sb, n_gdn = w_in_ab.shape[0], w_in_gdn.shape[0]
    dt = x_prompt.dtype

    R = -(-(BP + BS) // 8) * 8
    c_all = jnp.concatenate([c_prompt, c_sample, jnp.zeros((R - BP - BS, D), dt)], axis=0)
    mod_all = ada_modulation(c_all, w_ada, b_ada)
    mod_p = _Mod(mod_all[:, :, :BP].reshape(L, 9, BP, 1, D), T, False)
    mod_s = _Mod(jnp.repeat(mod_all[:, :, BP:BP + BS], TS, axis=2), TS, True)

    qkvz = conv_c_w.shape[-1] + w_out_gdn.shape[1]
    P = dict(
        norm_pre=norm_pre.reshape(L, 3, 1, D), norm_post=norm_post.reshape(L, 3, 1, D),
        wg=w_ffn_gate.astype(BF16), wu=w_ffn_up.astype(BF16), wd=w_ffn_down.astype(BF16),
        w_in_ab=w_in_ab.astype(BF16), w_out_ab=w_out_ab.astype(BF16),
        w_in_gdn=w_in_gdn.astype(BF16), w_in_gdn_ba=w_in_gdn[:, :, qkvz:].astype(BF16),
        w_out_gdn=w_out_gdn.astype(BF16),
        conv_a_w=conv_a_w, conv_a_b=conv_a_b, ln_a_g=ln_a_g, ln_a_b=ln_a_b, sb_bias=sb_bias,
        conv_c_w=conv_c_w, a_log=a_log, dt_bias=dt_bias, gdn_norm_g=gdn_norm_g,
    )

    y_p, kv_p, ca_p, cc_p, g_p = _trunk(
        x_prompt.reshape(BP * T, D), mod_p, BP, T, _Cfg(True), cache_kv_sb, None,
        jnp.zeros((n_sb, BP) + state_conv_a.shape[2:], dt), jnp.zeros((n_gdn, BP) + state_conv_c.shape[2:], dt),
        jnp.zeros((n_gdn, BP) + state_gdn.shape[2:], dt), P)
    y_s, kv_s, ca_s, cc_s, g_s = _trunk(
        x_sample.reshape(BS * TS, D), mod_s, BS, TS, _Cfg(False), cache_kv_sb, page_table,
        state_conv_a, state_conv_c, state_gdn, P)
    return (y_p, y_s, kv_p, ca_p, cc_p, g_p, kv_s, ca_s, cc_s, g_s)
```

```python
import functools
import math

import jax
import jax.numpy as jnp
from jax import lax
from jax.experimental import pallas as pl
from jax.experimental.pallas import tpu as pltpu

F32 = jnp.float32
BF16 = jnp.bfloat16
EPS = 1e-6
FFN_RESIDUAL = 0.5
VMEM_LIMIT = 56 * 1024 * 1024


def _cparams(sem):
    return pltpu.CompilerParams(dimension_semantics=sem, vmem_limit_bytes=VMEM_LIMIT)


def _silu(x):
    return x * jax.nn.sigmoid(x)


def _rms(x):
    return x * lax.rsqrt(jnp.mean(x * x, axis=-1, keepdims=True) + EPS)


ROWS = 16
ROWS_UNROLL = 4


def _rows_loop(n_rows, body):
    def step(r, carry):
        body(pl.multiple_of(r * ROWS, ROWS))
        return carry
    n = n_rows // ROWS
    lax.fori_loop(0, n, step, 0, unroll=math.gcd(n, ROWS_UNROLL))


def _mod_rows(ref, r0):
    return ref[...] if ref.shape[0] == 1 else ref[pl.ds(r0, ROWS), :]


def _prenorm_rows(x_ref, sh_ref, sc_ref, gpre_ref, h_ref):
    def body(r0):
        y = _rms(x_ref[pl.ds(r0, ROWS), :]) * gpre_ref[...]
        h_ref[pl.ds(r0, ROWS), :] = (y * (1.0 + _mod_rows(sc_ref, r0)) + _mod_rows(sh_ref, r0)).astype(BF16)
    _rows_loop(x_ref.shape[0], body)


def _postnorm_rows(acc_ref, x_ref, gt_ref, gpost_ref, o_ref, weight):
    def body(r0):
        rs = pl.ds(r0, ROWS)
        y = _rms(acc_ref[rs, :]) * gpost_ref[...]
        o_ref[rs, :] = x_ref[rs, :] + weight * (1.0 + _mod_rows(gt_ref, r0)) * y
    _rows_loop(x_ref.shape[0], body)


def _ada_kernel(c_ref, w_ref, b_ref, o_ref):
    s = _silu(c_ref[...]).astype(BF16)
    o_ref[...] = jnp.dot(s, w_ref[...].astype(BF16), preferred_element_type=F32) + b_ref[...]


def ada_modulation(c_all, w_ada, b_ada):
    L, D, N = w_ada.shape
    R = c_all.shape[0]
    nj = N // D
    return pl.pallas_call(
        _ada_kernel,
        grid=(L, nj),
        in_specs=[
            pl.BlockSpec((R, D), lambda l, j: (0, 0)),
            pl.BlockSpec((None, D, D), lambda l, j: (l, 0, j)),
            pl.BlockSpec((None, 1, D), lambda l, j: (l, 0, j)),
        ],
        out_specs=pl.BlockSpec((None, None, R, D), lambda l, j: (l, j, 0, 0)),
        out_shape=jax.ShapeDtypeStruct((L, nj, R, D), F32),
        compiler_params=_cparams(("parallel", "parallel")),
        name="ada_modulation",
    )(c_all, w_ada, b_ada.reshape(L, 1, N))


class _Mod:
    def __init__(self, arr, rows_per_seq, per_row):
        self.arr = arr
        self.rows_per_seq = rows_per_seq
        self.per_row = per_row

    def spec(self, l, k, tm, ngrid):
        D = self.arr.shape[-1]
        if self.per_row:
            def imap(i, *_):
                return (l, k, i, 0)
            return pl.BlockSpec((None, None, tm, D), imap)
        rps = self.rows_per_seq

        def imap(i, *_):
            return (l, k, (i * tm) // rps, 0, 0)
        return pl.BlockSpec((None, None, None, 1, D), imap)


def _vec_spec(l, s, D):
    return pl.BlockSpec((None, None, 1, D), lambda i, *_: (l, s, 0, 0))


def _ffn_kernel(x_ref, sh_ref, sc_ref, gt_ref, gpre_ref, gpost_ref, wg_ref, wu_ref, wd_ref,
                o_ref, h_ref, acc_ref, *, weight):
    f = pl.program_id(1)

    @pl.when(f == 0)
    def _():
        y = _rms(x_ref[...]) * gpre_ref[...]
        h_ref[...] = (y * (1.0 + sc_ref[...]) + sh_ref[...]).astype(BF16)
        acc_ref[...] = jnp.zeros_like(acc_ref)

    h = h_ref[...]
    g = jnp.dot(h, wg_ref[...], preferred_element_type=F32)
    u = jnp.dot(h, wu_ref[...], preferred_element_type=F32)
    a = (_silu(g) * u).astype(BF16)
    acc_ref[...] += jnp.dot(a, wd_ref[...], preferred_element_type=F32)

    @pl.when(f == pl.num_programs(1) - 1)
    def _():
        y = _rms(acc_ref[...]) * gpost_ref[...]
        o_ref[...] = x_ref[...] + weight * (1.0 + gt_ref[...]) * y


def ffn_block(x, mod, norm_pre, norm_post, wg, wu, wd, l, s, half, tm, tf):
    M, D = x.shape
    F = wg.shape[-1]
    grid = (M // tm, F // tf)
    k0 = 3 * s
    return pl.pallas_call(
        functools.partial(_ffn_kernel, weight=FFN_RESIDUAL),
        grid=grid,
        in_specs=[
            pl.BlockSpec((tm, D), lambda i, f: (i, 0)),
            mod.spec(l, k0 + 0, tm, grid), mod.spec(l, k0 + 1, tm, grid), mod.spec(l, k0 + 2, tm, grid),
            _vec_spec(l, s, D), _vec_spec(l, s, D),
            pl.BlockSpec((None, None, D, tf), lambda i, f: (l, half, 0, f)),
            pl.BlockSpec((None, None, D, tf), lambda i, f: (l, half, 0, f)),
            pl.BlockSpec((None, None, tf, D), lambda i, f: (l, half, f, 0)),
        ],
        out_specs=pl.BlockSpec((tm, D), lambda i, f: (i, 0)),
        out_shape=jax.ShapeDtypeStruct((M, D), F32),
        scratch_shapes=[pltpu.VMEM((tm, D), BF16), pltpu.VMEM((tm, D), F32)],
        compiler_params=_cparams(("parallel", "arbitrary")),
        name="ffn_block",
    )(x, mod.arr, mod.arr, mod.arr, norm_pre, norm_post, wg, wu, wd)


def _inproj_kernel(x_ref, sh_ref, sc_ref, gpre_ref, w_ref, *rest, has_extra):
    if has_extra:
        we_ref, o_ref, oe_ref, h_ref = rest
    else:
        o_ref, h_ref = rest
    j = pl.program_id(1)

    @pl.when(j == 0)
    def _():
        _prenorm_rows(x_ref, sh_ref, sc_ref, gpre_ref, h_ref)
        if has_extra:
            oe_ref[...] = jnp.dot(h_ref[...], we_ref[...], preferred_element_type=F32)

    o_ref[...] = jnp.dot(h_ref[...], w_ref[...], preferred_element_type=F32)


def inproj_block(x, mod, norm_pre, w, l, s, tm, tn, w_extra=None):
    M, D = x.shape
    N = w.shape[-1] // tn * tn
    i_kind = l // 2
    grid = (M // tm, N // tn)
    k0 = 3 * s
    in_specs = [
        pl.BlockSpec((tm, D), lambda i, j: (i, 0)),
        mod.spec(l, k0 + 0, tm, grid), mod.spec(l, k0 + 1, tm, grid),
        _vec_spec(l, s, D),
        pl.BlockSpec((None, D, tn), lambda i, j: (i_kind, 0, j)),
    ]
    args = [x, mod.arr, mod.arr, norm_pre, w]
    out_specs = [pl.BlockSpec((tm, tn), lambda i, j: (i, j))]
    out_shape = [jax.ShapeDtypeStruct((M, N), F32)]
    if w_extra is not None:
        ne = w_extra.shape[-1]
        in_specs.append(pl.BlockSpec((None, D, ne), lambda i, j: (i_kind, 0, 0)))
        args.append(w_extra)
        out_specs.append(pl.BlockSpec((tm, ne), lambda i, j: (i, 0)))
        out_shape.append(jax.ShapeDtypeStruct((M, ne), F32))
    res = pl.pallas_call(
        functools.partial(_inproj_kernel, has_extra=w_extra is not None),
        grid=grid,
        in_specs=in_specs,
        out_specs=out_specs,
        out_shape=out_shape,
        scratch_shapes=[pltpu.VMEM((tm, D), BF16)],
        compiler_params=_cparams(("parallel", "arbitrary")),
        name="inproj_block",
    )(*args)
    return res if w_extra is not None else res[0]


def _outproj_kernel(x_ref, gt_ref, gpost_ref, w_ref, *rest, starts):
    a_refs, (o_ref, acc_ref) = rest[:-2], rest[-2:]
    k = pl.program_id(1)

    @pl.when(k == 0)
    def _():
        acc_ref[...] = jnp.zeros_like(acc_ref)

    for p, a_ref in enumerate(a_refs):
        @pl.when((k >= starts[p]) & (k < starts[p + 1]))
        def _():
            acc_ref[...] += jnp.dot(a_ref[...], w_ref[...], preferred_element_type=F32)

    @pl.when(k == pl.num_programs(1) - 1)
    def _():
        _postnorm_rows(acc_ref, x_ref, gt_ref, gpost_ref, o_ref, 1.0)


def outproj_block(x, a_parts, mod, norm_post, w, l, s, tm, tk):
    M, D = x.shape
    i_kind = l // 2
    starts = [0]
    for a in a_parts:
        starts.append(starts[-1] + a.shape[-1] // tk)
    grid = (M // tm, starts[-1])

    def a_spec(p):
        lo, n = starts[p], starts[p + 1] - starts[p]
        return pl.BlockSpec((tm, tk), lambda i, k: (i, jnp.clip(k - lo, 0, n - 1)))

    return pl.pallas_call(
        functools.partial(_outproj_kernel, starts=tuple(starts)),
        grid=grid,
        in_specs=[
            pl.BlockSpec((tm, D), lambda i, k: (i, 0)),
            mod.spec(l, 3 * s + 2, tm, grid),
            _vec_spec(l, s, D),
            pl.BlockSpec((None, tk, D), lambda i, k: (i_kind, k, 0)),
        ] + [a_spec(p) for p in range(len(a_parts))],
        out_specs=pl.BlockSpec((tm, D), lambda i, k: (i, 0)),
        out_shape=jax.ShapeDtypeStruct((M, D), F32),
        scratch_shapes=[pltpu.VMEM((tm, D), F32)],
        compiler_params=_cparams(("parallel", "arbitrary")),
        name="outproj_block",
    )(x, mod.arr, norm_post, w, *a_parts)


CONV_HIST = 32
LANES = 128


def _shifted_taps(ext_ref, w_ref, r0, cs, rc, hist_pad, off, taps, acc):
    n = rc + hist_pad
    win = ext_ref[pl.ds(r0, n), cs]
    for phase in range(8):
        wb = None
        for a8 in range(hist_pad // 8 + 1):
            j = 8 * a8 + phase - off
            if 0 <= j < taps:
                if wb is None:
                    wb = win if phase == 0 else pltpu.roll(win, n - phase, 0)
                acc = acc + w_ref[j:j + 1, cs] * wb[8 * a8:8 * a8 + rc]
    return acc


def _conv_a_kernel(val_ref, gate_ref, st_ref, w_ref, b_ref, lg_ref, lb_ref, a_ref, ns_ref,
                   ext_ref, y_ref, *, tt, rc, taps):
    t = pl.program_id(1)
    hist = taps - 1
    off = CONV_HIST - hist
    ch = val_ref.shape[-1]

    @pl.when(t == 0)
    def _():
        ext_ref[off:CONV_HIST, :] = st_ref[...]

    @pl.when(t > 0)
    def _():
        ext_ref[0:CONV_HIST, :] = ext_ref[tt:tt + CONV_HIST, :]

    ext_ref[CONV_HIST:CONV_HIST + tt, :] = val_ref[...] * jax.nn.sigmoid(gate_ref[...])
    if tt % rc:
        ext_ref[CONV_HIST + tt:, :] = jnp.zeros((ext_ref.shape[0] - CONV_HIST - tt, ch), F32)

    def chunk(r, carry):
        r0 = pl.multiple_of(r * rc, rc)
        for c in range(ch // LANES):
            cs = slice(c * LANES, (c + 1) * LANES)
            acc = _shifted_taps(ext_ref, w_ref, r0, cs, rc, CONV_HIST, off, taps,
                                jnp.broadcast_to(b_ref[:, cs], (rc, LANES)))
            rows = min(rc, tt)
            y_ref[pl.ds(r0, rows), cs] = acc[:rows]
        return carry

    lax.fori_loop(0, pl.cdiv(tt, rc), chunk, 0)

    y = y_ref[...]
    mu = jnp.mean(y, axis=-1, keepdims=True)
    d = y - mu
    var = jnp.mean(d * d, axis=-1, keepdims=True)
    a_ref[...] = _silu(d * lax.rsqrt(var + EPS) * lg_ref[...] + lb_ref[...]).astype(a_ref.dtype)

    @pl.when(t == pl.num_programs(1) - 1)
    def _():
        ns_ref[...] = ext_ref[tt + off:tt + CONV_HIST, :]


def conv_a_block(proj3, conv_state, conv_w, conv_b, ln_g, ln_b, tt, rc):
    B, T, _ = proj3.shape
    taps, CH = conv_w.shape
    row = lambda v: v.reshape(1, CH)
    return pl.pallas_call(
        functools.partial(_conv_a_kernel, tt=tt, rc=rc, taps=taps),
        grid=(B, T // tt),
        in_specs=[
            pl.BlockSpec((None, tt, CH), lambda b, t: (b, t, 0)),
            pl.BlockSpec((None, tt, CH), lambda b, t: (b, t, 1)),
            pl.BlockSpec((None, taps - 1, CH), lambda b, t: (b, 0, 0)),
            pl.BlockSpec((taps, CH), lambda b, t: (0, 0)),
            pl.BlockSpec((1, CH), lambda b, t: (0, 0)),
            pl.BlockSpec((1, CH), lambda b, t: (0, 0)),
            pl.BlockSpec((1, CH), lambda b, t: (0, 0)),
        ],
        out_specs=[
            pl.BlockSpec((None, tt, CH), lambda b, t: (b, t, 0)),
            pl.BlockSpec((None, taps - 1, CH), lambda b, t: (b, 0, 0)),
        ],
        out_shape=[jax.ShapeDtypeStruct((B, T, CH), BF16),
                   jax.ShapeDtypeStruct((B, taps - 1, CH), F32)],
        scratch_shapes=[pltpu.VMEM((CONV_HIST + pl.cdiv(tt, rc) * rc, CH), F32), pltpu.VMEM((tt, CH), F32)],
        compiler_params=_cparams(("parallel", "arbitrary")),
        name="conv_a_block",
    )(proj3, proj3, conv_state, conv_w, row(conv_b), row(ln_g), row(ln_b))


def _log_sigmoid_pair(z):
    l = jnp.log(1.0 + jnp.exp(-jnp.abs(z)))
    return jnp.minimum(z, 0.0) - l, jnp.minimum(-z, 0.0) - l


def _suffix_sum(lk, u2_ref):
    hi = lk.astype(BF16)
    lo = (lk - hi.astype(F32)).astype(BF16)
    return jnp.dot(jnp.concatenate([hi, lo], axis=1), u2_ref[...], preferred_element_type=F32)


def _sb_prompt_kernel(bias_ref, q_ref, k_ref, v_ref, u_ref, o_ref, k16_ref, v16_ref, *, tq, tk, scale):
    h = pl.program_id(1)
    i = pl.program_id(2)

    @pl.when(i == 0)
    def _():
        k16_ref[...] = k_ref[...].astype(BF16)
        v16_ref[...] = v_ref[...].astype(BF16)

    bias = bias_ref[h]
    q16 = q_ref[...].astype(BF16)
    nd = tq // tk

    def sweep(j_hi, m, acc, diagonal):
        starts = [j_hi - (n + 1) * tk for n in range(nd)]
        first = [(nd - 1 - n) * tk if diagonal else 0 for n in range(nd)]
        lss, lks = [], []
        for n, j0 in enumerate(starts):
            kb = k16_ref[pl.ds(j0, tk), :]
            z = lax.dot_general(q16[first[n]:], kb, (((1,), (1,)), ((), ())),
                                preferred_element_type=F32) * scale + bias
            ls, lk = _log_sigmoid_pair(z)
            if diagonal:
                qpos = lax.broadcasted_iota(jnp.int32, z.shape, 0)
                kpos = lax.broadcasted_iota(jnp.int32, z.shape, 1)
                vis = kpos < qpos
                ls = jnp.where(vis, ls, -jnp.inf)
                lk = jnp.where(vis, lk, 0.0)
            lss.append(ls)
            lks.append(lk)
        laters = [_suffix_sum(lk, u_ref) for lk in lks]
        for n, j0 in enumerate(starts):
            r0 = first[n]
            w = jnp.exp(lss[n] + laters[n] + m[r0:])
            pv = jnp.dot(w.astype(BF16), v16_ref[pl.ds(j0, tk), :], preferred_element_type=F32)
            dm = laters[n][:, 0:1] + lks[n][:, 0:1]
            if r0:
                acc = jnp.concatenate([acc[:r0], acc[r0:] + pv], axis=0)
                m = jnp.concatenate([m[:r0], m[r0:] + dm], axis=0)
            else:
                acc, m = acc + pv, m + dm
        return m, acc

    base = pl.multiple_of(i * tq, tq)
    m, acc = sweep(base + tq, jnp.zeros((tq, 1), F32), jnp.zeros((tq, o_ref.shape[-1]), F32), True)

    def body(jj, carry):
        return sweep(pl.multiple_of(base - jj * tq, tq), carry[0], carry[1], False)

    m, acc = lax.fori_loop(0, i, body, (m, acc))
    o_ref[...] = acc.astype(o_ref.dtype)


def _suffix_matrix(n):
    r = lax.broadcasted_iota(jnp.int32, (2 * n, n), 0) % n
    c = lax.broadcasted_iota(jnp.int32, (2 * n, n), 1)
    return (r > c).astype(BF16)


def sb_attention_prompt(proj3, sb_bias, tq, tk, heads, q_col, k_col, v_col):
    B, T, _ = proj3.shape
    Dh = LANES
    return pl.pallas_call(
        functools.partial(_sb_prompt_kernel, tq=tq, tk=tk, scale=Dh ** -0.5),
        grid=(B, heads, T // tq),
        in_specs=[
            pl.BlockSpec(memory_space=pltpu.SMEM),
            pl.BlockSpec((None, tq, Dh), lambda b, h, i: (b, i, q_col + h)),
            pl.BlockSpec((None, T, Dh), lambda b, h, i: (b, 0, k_col + h)),
            pl.BlockSpec((None, T, Dh), lambda b, h, i: (b, 0, v_col + h)),
            pl.BlockSpec((2 * tk, tk), lambda b, h, i: (0, 0)),
        ],
        out_specs=pl.BlockSpec((None, tq, Dh), lambda b, h, i: (b, i, h)),
        out_shape=jax.ShapeDtypeStruct((B, T, heads * Dh), BF16),
        scratch_shapes=[pltpu.VMEM((T, Dh), BF16), pltpu.VMEM((T, Dh), BF16)],
        compiler_params=_cparams(("parallel", "parallel", "arbitrary")),
        name="sb_attention_prompt",
    )(sb_bias, proj3, proj3, proj3, _suffix_matrix(tk))


def _sb_sample_kernel(tbl_ref, q_ref, knew_ref, vnew_ref, bias_ref, u_ref, *rest, pp, page, heads, scale):
    page_refs = rest[:pp]
    o_ref, m_ref = rest[pp], rest[pp + 1]
    g = pl.program_id(1)
    R = q_ref.shape[0]
    q16 = q_ref[...].astype(BF16)
    bias = bias_ref[...]
    rhead = lax.broadcasted_iota(jnp.int32, (R, LANES), 0) % heads
    rtime = lax.broadcasted_iota(jnp.int32, (R, LANES), 0) // heads
    ccol = lax.broadcasted_iota(jnp.int32, (R, LANES), 1)
    same_head = (ccol % heads) == rhead

    def sweep(blocks, m, acc, vis):
        nch = blocks[0][0].shape[0] // LANES
        order = [(b, c) for b in range(len(blocks)) for c in reversed(range(nch))]
        lss, lks = {}, {}
        for b, (k2d, _) in enumerate(blocks):
            z = lax.dot_general(q16, k2d.astype(BF16), (((1,), (1,)), ((), ())),
                                preferred_element_type=F32) * scale + bias
            ls, lk = _log_sigmoid_pair(z)
            for c in range(nch):
                lss[b, c] = jnp.where(vis, ls[:, c * LANES:(c + 1) * LANES], -jnp.inf)
                lks[b, c] = jnp.where(vis, lk[:, c * LANES:(c + 1) * LANES], 0.0)
        later = _suffix_sum(jnp.concatenate([lks[bc] for bc in order], axis=0), u_ref)
        ws = {}
        for n_, bc in enumerate(order):
            lat = later[n_ * R:(n_ + 1) * R]
            ws[bc] = jnp.exp(lss[bc] + lat + m).astype(BF16)
            m = m + lat[:, 0:1] + lks[bc][:, 0:1]
        for b, (_, v2d) in enumerate(blocks):
            w = jnp.concatenate([ws[b, c] for c in range(nch)], axis=1)
            acc = acc + jnp.dot(w, v2d.astype(BF16), preferred_element_type=F32)
        return m, acc

    @pl.when(g == 0)
    def _():
        vis_new = same_head & ((ccol // heads) < rtime)
        m, acc = sweep([(knew_ref[...], vnew_ref[...])], jnp.zeros((R, 1), F32),
                       jnp.zeros(o_ref.shape, F32), vis_new)
        m_ref[...] = m
        o_ref[...] = acc

    blocks = [(pr[:, 0].reshape(page * heads, LANES), pr[:, 1].reshape(page * heads, LANES))
              for pr in page_refs]
    m, acc = sweep(blocks, m_ref[...], o_ref[...], same_head)
    m_ref[...] = m
    o_ref[...] = acc


def sb_attention_sample(q, k_new, v_new, cache_kv, layer_idx, page_table, sb_bias, pp):
    B, R, Dh = q.shape
    _, _, page, _, heads, _ = cache_kv.shape
    n_pages = page_table.shape[1]
    bias_col = jnp.tile(sb_bias, R // heads).reshape(R, 1)

    def page_spec(u):
        def imap(b, g, tbl):
            return (layer_idx, tbl[b, n_pages - 1 - (g * pp + u)], 0, 0, 0, 0)
        return pl.BlockSpec((None, None, page, 2, heads, Dh), imap)

    grid_spec = pltpu.PrefetchScalarGridSpec(
        num_scalar_prefetch=1,
        grid=(B, n_pages // pp),
        in_specs=[
            pl.BlockSpec((None, R, Dh), lambda b, g, tbl: (b, 0, 0)),
            pl.BlockSpec((None, LANES, Dh), lambda b, g, tbl: (b, 0, 0)),
            pl.BlockSpec((None, LANES, Dh), lambda b, g, tbl: (b, 0, 0)),
            pl.BlockSpec((R, 1), lambda b, g, tbl: (0, 0)),
            pl.BlockSpec((2 * LANES, LANES), lambda b, g, tbl: (0, 0)),
        ] + [page_spec(u) for u in range(pp)],
        out_specs=pl.BlockSpec((None, R, Dh), lambda b, g, tbl: (b, 0, 0)),
        scratch_shapes=[pltpu.VMEM((R, 1), F32)],
    )
    return pl.pallas_call(
        functools.partial(_sb_sample_kernel, pp=pp, page=page, heads=heads, scale=Dh ** -0.5),
        grid_spec=grid_spec,
        out_shape=jax.ShapeDtypeStruct((B, R, Dh), F32),
        compiler_params=_cparams(("parallel", "arbitrary")),
        name="sb_attention_sample",
    )(page_table, q, k_new, v_new, bias_col, _suffix_matrix(LANES), *([cache_kv] * pp))


GDN_HIST = 8


def _gdn_prep_kernel(x_ref, st_ref, w_ref, o_ref, ns_ref, ext_ref, *, tt, rc, taps, n_q, n_k, scale):
    j = pl.program_id(1)
    t = pl.program_id(2)
    hist = taps - 1
    off = GDN_HIST - hist
    tc = x_ref.shape[-1]

    @pl.when(t == 0)
    def _():
        ext_ref[off:GDN_HIST, :] = st_ref[...]

    @pl.when(t > 0)
    def _():
        ext_ref[0:GDN_HIST, :] = ext_ref[tt:tt + GDN_HIST, :]

    ext_ref[GDN_HIST:GDN_HIST + tt, :] = x_ref[...]
    if tt % rc:
        ext_ref[GDN_HIST + tt:, :] = jnp.zeros((ext_ref.shape[0] - GDN_HIST - tt, tc), F32)

    rows = min(rc, tt)

    def run(normalise, post_scale):
        def chunk(r, carry):
            r0 = pl.multiple_of(r * rc, rc)
            for c in range(tc // LANES):
                cs = slice(c * LANES, (c + 1) * LANES)
                y = _silu(_shifted_taps(ext_ref, w_ref, r0, cs, rc, GDN_HIST, off, taps,
                                        jnp.zeros((rc, LANES), F32)))
                if normalise:
                    y = y * lax.rsqrt(jnp.sum(y * y, axis=-1, keepdims=True) + EPS)
                    if post_scale != 1.0:
                        y = y * post_scale
                o_ref[pl.ds(r0, rows), cs] = y[:rows]
            return carry
        lax.fori_loop(0, pl.cdiv(tt, rc), chunk, 0)

    @pl.when(j < n_q)
    def _():
        run(True, scale)

    @pl.when((j >= n_q) & (j < n_q + n_k))
    def _():
        run(True, 1.0)

    @pl.when(j >= n_q + n_k)
    def _():
        run(False, 1.0)

    @pl.when(t == pl.num_programs(2) - 1)
    def _():
        ns_ref[...] = ext_ref[tt + off:tt + GDN_HIST, :]


def gdn_prep_block(proj3, conv_state, conv_w, kw, tt, rc, tc):
    B, T, _ = proj3.shape
    taps, QKV = conv_w.shape
    return pl.pallas_call(
        functools.partial(_gdn_prep_kernel, tt=tt, rc=rc, taps=taps, n_q=kw // tc, n_k=kw // tc,
                          scale=LANES ** -0.5),
        grid=(B, QKV // tc, T // tt),
        in_specs=[
            pl.BlockSpec((None, tt, tc), lambda b, j, t: (b, t, j)),
            pl.BlockSpec((None, taps - 1, tc), lambda b, j, t: (b, 0, j)),
            pl.BlockSpec((taps, tc), lambda b, j, t: (0, j)),
        ],
        out_specs=[
            pl.BlockSpec((None, tt, tc), lambda b, j, t: (b, t, j)),
            pl.BlockSpec((None, taps - 1, tc), lambda b, j, t: (b, 0, j)),
        ],
        out_shape=[jax.ShapeDtypeStruct((B, T, QKV), F32),
                   jax.ShapeDtypeStruct((B, taps - 1, QKV), F32)],
        scratch_shapes=[pltpu.VMEM((GDN_HIST + pl.cdiv(tt, rc) * rc, tc), F32)],
        compiler_params=_cparams(("parallel", "parallel", "arbitrary")),
        name="gdn_prep_block",
    )(proj3, conv_state, conv_w)


def _split3(x):
    hi = x.astype(BF16)
    r = x - hi.astype(F32)
    mid = r.astype(BF16)
    lo = (r - mid.astype(F32)).astype(BF16)
    return hi, mid, lo


def _gdn_gates_kernel(ba_ref, alog_ref, dtb_ref, tri_ref, o_ref, *, nh, seq_rows, valid_rows):
    ba = ba_ref[...]
    R = ba.shape[0]
    beta = jax.nn.sigmoid(ba[:, :nh])
    x = ba[:, nh:] + dtb_ref[...]
    softplus = jnp.maximum(x, 0.0) + jnp.log1p(jnp.exp(-jnp.abs(x)))
    g = -jnp.exp(alog_ref[...]) * softplus
    if valid_rows < seq_rows:
        row = lax.broadcasted_iota(jnp.int32, (R, nh), 0) % seq_rows
        beta = jnp.where(row < valid_rows, beta, 0.0)
        g = jnp.where(row < valid_rows, g, 0.0)
    tri = tri_ref[...]
    gc = sum(jnp.dot(tri, p, preferred_element_type=F32) for p in _split3(g))
    o_ref[...] = jnp.concatenate([gc, beta, jnp.zeros((R, LANES - 2 * nh), F32)], axis=1)


def gdn_gates(ba, a_log, dt_bias, chunk, seq_rows, valid_rows, tr):
    M, two_nh = ba.shape
    nh = two_nh // 2
    r = lax.broadcasted_iota(jnp.int32, (tr, tr), 0)
    c = lax.broadcasted_iota(jnp.int32, (tr, tr), 1)
    tri = ((r // chunk == c // chunk) & (r >= c)).astype(BF16)
    return pl.pallas_call(
        functools.partial(_gdn_gates_kernel, nh=nh, seq_rows=seq_rows, valid_rows=valid_rows),
        grid=(M // tr,),
        in_specs=[
            pl.BlockSpec((tr, two_nh), lambda i: (i, 0)),
            pl.BlockSpec((1, nh), lambda i: (0, 0)),
            pl.BlockSpec((1, nh), lambda i: (0, 0)),
            pl.BlockSpec((tr, tr), lambda i: (0, 0)),
        ],
        out_specs=pl.BlockSpec((tr, LANES), lambda i: (i, 0)),
        out_shape=jax.ShapeDtypeStruct((M, LANES), F32),
        compiler_params=_cparams(("parallel",)),
        name="gdn_gates",
    )(ba, a_log.reshape(1, nh), dt_bias.reshape(1, nh), tri)


def _gdn_chunk_kernel(q_ref, k_ref, v_ref, z_ref, gc_ref, bt_ref, s0_ref, ng_ref, o_ref, so_ref, s_ref,
                      *, nc, hk, C):
    t = pl.program_id(2)
    Dh = LANES
    C2 = 2 * C

    @pl.when(t == 0)
    def _():
        s_ref[...] = s0_ref[...]

    ii = lax.broadcasted_iota(jnp.int32, (C, C2), 0)
    ll = lax.broadcasted_iota(jnp.int32, (C, C2), 1)
    jj = ll % C
    left = ll < C
    causal = ii >= jj
    strict = ii > jj
    diag = ii == jj
    diag_of = [ii == ll, ii + C == ll]
    half_of = [left, ll >= C]
    eye2 = diag.astype(F32)
    nt = (((1,), (1,)), ((), ()))
    tn = (((0,), (0,)), ((), ()))

    def block_diag(p2):
        return jnp.concatenate([jnp.where(left, p2, 0.0), jnp.where(left, 0.0, p2)], axis=0).astype(BF16)

    units = [(kk, c) for kk in range(hk) for c in range(nc)]
    kq, gcol, bcol, grow2, decay2, attn2, pw, xinv = {}, {}, {}, {}, {}, {}, {}, {}

    for un in units:
        kk, c = un
        rs = slice(c * C, (c + 1) * C)
        ks = slice(kk * Dh, (kk + 1) * Dh)
        kc, qc = k_ref[rs, ks], q_ref[rs, ks]
        kq[un] = (kc, qc)
        k16 = kc.astype(BF16)
        k16x2 = jnp.concatenate([k16, k16], axis=0)
        gkk2 = lax.dot_general(k16, k16x2, nt, preferred_element_type=F32)
        gqk2 = lax.dot_general(qc.astype(BF16), k16x2, nt, preferred_element_type=F32)
        g2 = jnp.concatenate([gc_ref[2 * kk, :, rs], gc_ref[2 * kk + 1, :, rs]], axis=1)
        b2 = jnp.concatenate([bt_ref[2 * kk, :, rs], bt_ref[2 * kk + 1, :, rs]], axis=1)
        gcol[un] = [jnp.sum(jnp.where(dm, g2, 0.0), axis=1, keepdims=True) for dm in diag_of]
        bcol[un] = [jnp.sum(jnp.where(dm, b2, 0.0), axis=1, keepdims=True) for dm in diag_of]
        gcol2 = jnp.where(left, gcol[un][0], gcol[un][1])
        bcol2 = jnp.where(left, bcol[un][0], bcol[un][1])
        grow2[un] = g2
        decay2[un] = jnp.exp(jnp.where(causal, gcol2 - g2, -jnp.inf))
        a2 = jnp.where(strict, gkk2 * bcol2 * decay2[un], 0.0)
        attn2[un] = gqk2 * decay2[un]
        pw[un] = a2
        xinv[un] = eye2 - a2

    k = 2
    while k < C:
        for un in units:
            pw[un] = jnp.dot(pw[un].astype(BF16), block_diag(pw[un]), preferred_element_type=F32)
        for un in units:
            xinv[un] = xinv[un] + jnp.dot(xinv[un].astype(BF16), block_diag(pw[un]),
                                          preferred_element_type=F32)
        k *= 2

    heads = [(kk, e) for kk in range(hk) for e in range(2)]
    uw, qd, kd, gl = {}, {}, {}, {}
    for un in units:
        kk, c = un
        rs = slice(c * C, (c + 1) * C)
        kc, qc = kq[un]
        rhs = []
        for e in range(2):
            vs = slice((2 * kk + e) * Dh, (2 * kk + e + 1) * Dh)
            bc, gcl = bcol[un][e], gcol[un][e]
            egc = jnp.exp(gcl)
            glast = grow2[un][:, e * C + C - 1:e * C + C]
            rhs.append(jnp.concatenate([v_ref[rs, vs] * bc, kc * (bc * egc)], axis=1).astype(BF16))
            qd[kk, e, c] = (qc * egc).astype(BF16)
            kd[kk, e, c] = (kc * jnp.exp(glast - gcl)).astype(BF16)
            gl[kk, e, c] = jnp.exp(glast)
        rhs2 = jnp.concatenate(rhs, axis=0)
        for e in range(2):
            uw[kk, e, c] = jnp.dot(jnp.where(half_of[e], xinv[un], 0.0).astype(BF16), rhs2,
                                   preferred_element_type=F32)
            attn2[kk, e, c] = jnp.where(half_of[e], attn2[un], 0.0).astype(BF16)

    S = {h: s_ref[2 * h[0] + h[1]] for h in heads}
    for c in range(nc):
        rs = slice(c * C, (c + 1) * C)
        s16 = {h: S[h].astype(BF16) for h in heads}
        v_new = {h: uw[h[0], h[1], c][:, :Dh]
                 - jnp.dot(uw[h[0], h[1], c][:, Dh:].astype(BF16), s16[h], preferred_element_type=F32)
                 for h in heads}
        for h in heads:
            kk, e = h
            S[h] = S[h] * gl[kk, e, c] + lax.dot_general(kd[kk, e, c], v_new[h].astype(BF16), tn,
                                                          preferred_element_type=F32)
        vn2 = {kk: jnp.concatenate([v_new[kk, 0], v_new[kk, 1]], axis=0).astype(BF16) for kk in range(hk)}
        for h in heads:
            kk, e = h
            vs = slice((2 * kk + e) * Dh, (2 * kk + e + 1) * Dh)
            o = (jnp.dot(qd[kk, e, c], s16[h], preferred_element_type=F32)
                 + jnp.dot(attn2[kk, e, c], vn2[kk], preferred_element_type=F32))
            o = _rms(o) * ng_ref[...] * _silu(z_ref[rs, vs])
            o_ref[rs, vs] = o.astype(o_ref.dtype)
    for h in heads:
        s_ref[2 * h[0] + h[1]] = S[h]

    @pl.when(t == pl.num_programs(2) - 1)
    def _():
        so_ref[...] = s_ref[...]


def gdn_chunk_block(qkv, proj3, gates_t, s0, norm_g, kw, z_col0, chunk, nc, hk):
    B, T, QKV = qkv.shape
    Dh = LANES
    nh = s0.shape[1]
    kh = kw // Dh
    tcs = nc * chunk
    qw, vw = hk * Dh, 2 * hk * Dh
    return pl.pallas_call(
        functools.partial(_gdn_chunk_kernel, nc=nc, hk=hk, C=chunk),
        grid=(B, kh // hk, T // tcs),
        in_specs=[
            pl.BlockSpec((None, tcs, qw), lambda b, g, t: (b, t, g)),
            pl.BlockSpec((None, tcs, qw), lambda b, g, t: (b, t, kw // qw + g)),
            pl.BlockSpec((None, tcs, vw), lambda b, g, t: (b, t, 2 * kw // vw + g)),
            pl.BlockSpec((None, tcs, vw), lambda b, g, t: (b, t, z_col0 // vw + g)),
            pl.BlockSpec((None, 2 * hk, 1, tcs), lambda b, g, t: (b, g, 0, t)),
            pl.BlockSpec((None, 2 * hk, 1, tcs), lambda b, g, t: (b, nh // (2 * hk) + g, 0, t)),
            pl.BlockSpec((None, 2 * hk, Dh, Dh), lambda b, g, t: (b, g, 0, 0)),
            pl.BlockSpec((1, Dh), lambda b, g, t: (0, 0)),
        ],
        out_specs=[
            pl.BlockSpec((None, tcs, vw), lambda b, g, t: (b, t, g)),
            pl.BlockSpec((None, 2 * hk, Dh, Dh), lambda b, g, t: (b, g, 0, 0)),
        ],
        out_shape=[jax.ShapeDtypeStruct((B, T, nh * Dh), BF16),
                   jax.ShapeDtypeStruct((B, nh, Dh, Dh), F32)],
        scratch_shapes=[pltpu.VMEM((2 * hk, Dh, Dh), F32)],
        compiler_params=_cparams(("parallel", "parallel", "arbitrary")),
        name="gdn_chunk_block",
    )(qkv, qkv, qkv, proj3, gates_t, gates_t, s0, norm_g.reshape(1, Dh))


GDN_CHUNK = 64


class _Cfg:
    def __init__(self, prompt):
        self.prompt = prompt
        if prompt:
            self.tm, self.tm_in, self.tf, self.conv_tt, self.conv_rc = 512, 1024, 512, 256, 64
            self.gdn_tt, self.gdn_rc, self.gdn_nc, self.gdn_hk = 256, 64, 4, 8
        else:
            self.tm, self.tm_in, self.tf, self.conv_tt, self.conv_rc = None, None, 1408, None, 8
            self.gdn_tt, self.gdn_rc, self.gdn_nc, self.gdn_hk = None, 8, 1, 4


def _conv_sb_mixer(proj3, i, cfg, conv_state, cache_kv, table, P):
    B, T, _ = proj3.shape
    ch = P['conv_a_w'].shape[-1]
    heads, Dh = cache_kv.shape[-2], cache_kv.shape[-1]
    sbw = heads * Dh
    qb, kb, vb = 2 * ch // Dh, (2 * ch + sbw) // Dh, (2 * ch + 2 * sbw) // Dh
    a16, cs = conv_a_block(proj3, conv_state, P['conv_a_w'][i], P['conv_a_b'][i], P['ln_a_g'][i],
                           P['ln_a_b'][i], cfg.conv_tt or T, cfg.conv_rc)
    kv = proj3[:, :, 2 * ch + sbw:].reshape(B, T, 2, heads, Dh)
    if cfg.prompt:
        o16 = sb_attention_prompt(proj3, P['sb_bias'][i], min(T, 1024), 256, heads, qb, kb, vb)
    else:
        rows = T * heads
        q = proj3[:, :, 2 * ch:2 * ch + sbw].reshape(B, rows, Dh)
        pad = lambda x: jnp.pad(x.reshape(B, rows, Dh), ((0, 0), (0, LANES - rows), (0, 0)))
        o = sb_attention_sample(q, pad(kv[:, :, 0]), pad(kv[:, :, 1]), cache_kv, i, table, P['sb_bias'][i], 8)
        o16 = o.reshape(B, T, sbw).astype(BF16)
    return [a16.reshape(B * T, ch), o16.reshape(B * T, sbw)], cs, kv


def _gdn_mixer(proj3, ba, i, cfg, conv_state, s0, P):
    B, T, _ = proj3.shape
    nh = s0.shape[1]
    kw = P['w_out_gdn'].shape[1] // 2
    qkv_w = P['conv_c_w'].shape[-1]
    qkv, cs = gdn_prep_block(proj3, conv_state, P['conv_c_w'][i], kw, cfg.gdn_tt or T, cfg.gdn_rc, 1024)
    if T % GDN_CHUNK:
        tp = -(-T // GDN_CHUNK) * GDN_CHUNK
        padt = lambda x: jnp.pad(x, ((0, 0), (0, tp - T), (0, 0)))
        qkv, zsrc, z_col0, ba = padt(qkv), padt(proj3[:, :, qkv_w:]), 0, padt(ba)
    else:
        tp, zsrc, z_col0 = T, proj3, qkv_w
    gates = gdn_gates(ba.reshape(B * tp, 2 * nh), P['a_log'][i], P['dt_bias'][i], GDN_CHUNK, tp, T,
                      min(512, B * tp))
    gates_t = gates.reshape(B, tp, LANES)[:, :, :2 * nh].transpose(0, 2, 1).reshape(B, 2 * nh, 1, tp)
    og, s = gdn_chunk_block(qkv, zsrc, gates_t, s0, P['gdn_norm_g'][i], kw, z_col0, GDN_CHUNK, cfg.gdn_nc,
                            cfg.gdn_hk)
    return [og[:, :T].reshape(B * T, -1)], cs, s


def _trunk(x, mod, B, T, cfg, cache_kv, table, conv_a_state, conv_c_state, gdn_state, P):
    D = x.shape[-1]
    depth = P['norm_pre'].shape[0]
    tm = cfg.tm or B * T
    tm_in = cfg.tm_in or B * T
    tf = cfg.tf
    kv_rows, conv_a_new, conv_c_new, gdn_new = [], [], [], []
    for l in range(depth):
        i = l // 2
        x = ffn_block(x, mod, P['norm_pre'], P['norm_post'], P['wg'], P['wu'], P['wd'], l, 0, 0, tm, tf)
        if l % 2 == 0:
            proj = inproj_block(x, mod, P['norm_pre'], P['w_in_ab'], l, 1, tm_in, 1024)
            cat, cs, kv = _conv_sb_mixer(proj.reshape(B, T, -1), i, cfg, conv_a_state[i], cache_kv, table, P)
            conv_a_new.append(cs)
            kv_rows.append(kv)
            x = outproj_block(x, cat, mod, P['norm_post'], P['w_out_ab'], l, 1, tm, 1024)
        else:
            proj, ba = inproj_block(x, mod, P['norm_pre'], P['w_in_gdn'], l, 1, tm_in, 1024,
                                    w_extra=P['w_in_gdn_ba'])
            og, cs, s = _gdn_mixer(proj.reshape(B, T, -1), ba.reshape(B, T, -1), i, cfg, conv_c_state[i],
                                   gdn_state[i], P)
            conv_c_new.append(cs)
            gdn_new.append(s)
            x = outproj_block(x, og, mod, P['norm_post'], P['w_out_gdn'], l, 1, tm, 1024)
        x = ffn_block(x, mod, P['norm_pre'], P['norm_post'], P['wg'], P['wu'], P['wd'], l, 2, 1, tm, tf)
    return (x.reshape(B, T, D), jnp.stack(kv_rows), jnp.stack(conv_a_new), jnp.stack(conv_c_new),
            jnp.stack(gdn_new))


def kernel(x_prompt, x_sample, cache_kv_sb, state_conv_a, state_conv_c, state_gdn, page_table,
           c_prompt, c_sample, w_ada, b_ada, norm_pre, norm_post, w_ffn_gate, w_ffn_up, w_ffn_down,
           w_in_ab, conv_a_w, conv_a_b, ln_a_g, ln_a_b, w_out_ab, sb_bias,
           w_in_gdn, conv_c_w, a_log, dt_bias, gdn_norm_g, w_out_gdn):
    BP, T, D = x_prompt.shape
    BS, TS, _ = x_sample.shape
    L = w_ada.shape[0]
    n_sb, n_gdn = w_in_ab.shape[0], w_in_gdn.shape[0]
    dt = x_prompt.dtype

    R = -(-(BP + BS) // 8) * 8
    c_all = jnp.concatenate([c_prompt, c_sample, jnp.zeros((R - BP - BS, D), dt)], axis=0)
    mod_all = ada_modulation(c_all, w_ada, b_ada)
    mod_p = _Mod(mod_all[:, :, :BP].reshape(L, 9, BP, 1, D), T, False)
    mod_s = _Mod(jnp.repeat(mod_all[:, :, BP:BP + BS], TS, axis=2), TS, True)

    qkvz = conv_c_w.shape[-1] + w_out_gdn.shape[1]
    P = dict(
        norm_pre=norm_pre.reshape(L, 3, 1, D), norm_post=norm_post.reshape(L, 3, 1, D),
        wg=w_ffn_gate.astype(BF16), wu=w_ffn_up.astype(BF16), wd=w_ffn_down.astype(BF16),
        w_in_ab=w_in_ab.astype(BF16), w_out_ab=w_out_ab.astype(BF16),
        w_in_gdn=w_in_gdn.astype(BF16), w_in_gdn_ba=w_in_gdn[:, :, qkvz:].astype(BF16),
        w_out_gdn=w_out_gdn.astype(BF16),
        conv_a_w=conv_a_w, conv_a_b=conv_a_b, ln_a_g=ln_a_g, ln_a_b=ln_a_b, sb_bias=sb_bias,
        conv_c_w=conv_c_w, a_log=a_log, dt_bias=dt_bias, gdn_norm_g=gdn_norm_g,
    )

    y_p, kv_p, ca_p, cc_p, g_p = _trunk(
        x_prompt.reshape(BP * T, D), mod_p, BP, T, _Cfg(True), cache_kv_sb, None,
        jnp.zeros((n_sb, BP) + state_conv_a.shape[2:], dt), jnp.zeros((n_gdn, BP) + state_conv_c.shape[2:], dt),
        jnp.zeros((n_gdn, BP) + state_gdn.shape[2:], dt), P)
    y_s, kv_s, ca_s, cc_s, g_s = _trunk(
        x_sample.reshape(BS * TS, D), mod_s, BS, TS, _Cfg(False), cache_kv_sb, page_table,
        state_conv_a, state_conv_c, state_gdn, P)
    return (y_p, y_s, kv_p, ca_p, cc_p, g_p, kv_s, ca_s, cc_s, g_s)
```
